```python
import jax, jax.numpy as jnp
from jax import lax
import numpy as np

D_MODEL = 1024
BATCH = 16
SEQ = 2048
DEPTH = 2

N_A_LAYERS = DEPTH // 2
N_B_LAYERS = DEPTH - N_A_LAYERS
D_FF = 2816
CHUNK = 128
GMLP_HALF = 2 * D_MODEL
GMLP_GROUPS = 16
GMLP_GROUP_DIM = GMLP_HALF // GMLP_GROUPS
N_HEADS = 8
QK_NOPE = 128
QK_ROPE = 64
V_DIM = 128
KV_RANK = 256
Q_RANK = 512
Q_BLOCK = 128
ROPE_THETA = 10000.0
RMS_EPS = 1e-6
LN_EPS = 1e-5
NEG_INF = -1e30

kernel_name = "yoco_gmlp_mla_macaron_sandwich"


def rms_norm(x, g):
    x32 = x.astype(jnp.float32)
    y = x32 * lax.rsqrt(jnp.mean(x32 * x32, axis=-1, keepdims=True) + RMS_EPS) * g.astype(jnp.float32)
    return y.astype(x.dtype)


def layer_norm(x, g, b):
    x32 = x.astype(jnp.float32)
    mu = jnp.mean(x32, axis=-1, keepdims=True)
    xc = x32 - mu
    y = xc * lax.rsqrt(jnp.mean(xc * xc, axis=-1, keepdims=True) + LN_EPS)
    return (y * g.astype(jnp.float32) + b.astype(jnp.float32)).astype(x.dtype)


def swiglu(n, w_gate, w_up, w_down):
    return (jax.nn.silu(n @ w_gate) * (n @ w_up)) @ w_down


def rope_tables(positions):
    inv_freq = ROPE_THETA ** (-jnp.arange(0, QK_ROPE, 2, dtype=jnp.float32) / QK_ROPE)
    ang = positions.astype(jnp.float32)[..., None] * inv_freq
    return jnp.cos(ang), jnp.sin(ang)


def apply_rope(x, cos, sin):
    cos = cos.astype(x.dtype)
    sin = sin.astype(x.dtype)
    x1, x2 = jnp.split(x, 2, axis=-1)
    return jnp.concatenate([x1 * cos - x2 * sin, x2 * cos + x1 * sin], axis=-1)


def gmlp_mixer(n, w_in, ln_g, ln_b, w_s, b_s, w_out):
    b, s, _ = n.shape
    z = jax.nn.gelu(n @ w_in)
    u, v = jnp.split(z, 2, axis=-1)
    v = layer_norm(v, ln_g, ln_b)
    v = v.reshape(b, s // CHUNK, CHUNK, GMLP_GROUPS, GMLP_GROUP_DIM)
    causal = jnp.tril(jnp.ones((CHUNK, CHUNK), dtype=w_s.dtype))
    w = w_s * causal
    sv = jnp.einsum('gtc,bncgd->bntgd', w, v) + jnp.transpose(b_s)[None, None, :, :, None]
    return (u * sv.reshape(b, s, GMLP_HALF)) @ w_out


def shared_kv(h, kv_norm_g, w_dkv, kv_a_norm_g, w_ukv, cos, sin):
    b, s, _ = h.shape
    n = rms_norm(h, kv_norm_g)
    ckv = n @ w_dkv
    c, k_rope = ckv[..., :KV_RANK], ckv[..., KV_RANK:]
    c = rms_norm(c, kv_a_norm_g)
    k_rope = apply_rope(k_rope, cos, sin)
    kv = (c @ w_ukv).reshape(b, s, N_HEADS, QK_NOPE + V_DIM)
    return kv[..., :QK_NOPE], k_rope, kv[..., QK_NOPE:]


def mla_mixer(n, k_nope, k_rope, v, w_dq, q_norm_g, w_uq, w_o, cos, sin):
    b, s, _ = n.shape
    q = (rms_norm(n @ w_dq, q_norm_g) @ w_uq).reshape(b, s, N_HEADS, QK_NOPE + QK_ROPE)
    q_nope = q[..., :QK_NOPE]
    q_rope = apply_rope(q[..., QK_NOPE:], cos[:, :, None, :], sin[:, :, None, :])
    scale = (QK_NOPE + QK_ROPE) ** -0.5
    outs = []
    for blk in range(s // Q_BLOCK):
        q0, q1 = blk * Q_BLOCK, (blk + 1) * Q_BLOCK
        sc = (jnp.einsum('bqhd,bkhd->bhqk', q_nope[:, q0:q1], k_nope[:, :q1])
              + jnp.einsum('bqhr,bkr->bhqk', q_rope[:, q0:q1], k_rope[:, :q1]))
        sc = sc.astype(jnp.float32) * scale
        q_idx = q0 + jnp.arange(Q_BLOCK)
        k_idx = jnp.arange(q1)
        sc = jnp.where(k_idx[None, :] <= q_idx[:, None], sc, NEG_INF)
        p = jax.nn.softmax(sc, axis=-1).astype(v.dtype)
        outs.append(jnp.einsum('bhqk,bkhd->bqhd', p, v[:, :q1]))
    o = jnp.concatenate(outs, axis=1).reshape(b, s, N_HEADS * V_DIM)
    return o @ w_o


def setup_inputs(seed: int = 0) -> dict:
    key = jax.random.key(seed)
    ks = iter(jax.random.split(key, 40))

    def w(shape, fan_in):
        return jax.random.normal(next(ks), shape, jnp.float32) * (fan_in ** -0.5)

    def g(shape):
        return 1.0 + 0.02 * jax.random.normal(next(ks), shape, jnp.float32)

    x = jax.random.normal(next(ks), (BATCH, SEQ, D_MODEL), jnp.float32)
    offs = jax.random.randint(next(ks), (BATCH, 1), 0, 1024, dtype=jnp.int32)
    positions = (jnp.arange(SEQ, dtype=jnp.int32)[None, :] + offs).astype(jnp.int32)
    return {
        "x": x,
        "positions": positions,
        "ffn_pre_g": g((DEPTH, 2, D_MODEL)),
        "ffn_post_g": g((DEPTH, 2, D_MODEL)),
        "ffn_w_gate": w((DEPTH, 2, D_MODEL, D_FF), D_MODEL),
        "ffn_w_up": w((DEPTH, 2, D_MODEL, D_FF), D_MODEL),
        "ffn_w_down": w((DEPTH, 2, D_FF, D_MODEL), D_FF),
        "mix_pre_g": g((DEPTH, D_MODEL)),
        "mix_post_g": g((DEPTH, D_MODEL)),
        "gmlp_w_in": w((N_A_LAYERS, D_MODEL, 2 * GMLP_HALF), D_MODEL),
        "gmlp_ln_g": g((N_A_LAYERS, GMLP_HALF)),
        "gmlp_ln_b": 0.02 * jax.random.normal(next(ks), (N_A_LAYERS, GMLP_HALF), jnp.float32),
        "gmlp_w_s": w((N_A_LAYERS, GMLP_GROUPS, CHUNK, CHUNK), CHUNK),
        "gmlp_b_s": g((N_A_LAYERS, GMLP_GROUPS, CHUNK)),
        "gmlp_w_out": w((N_A_LAYERS, GMLP_HALF, D_MODEL), GMLP_HALF),
        "kv_norm_g": g((D_MODEL,)),
        "w_dkv": w((D_MODEL, KV_RANK + QK_ROPE), D_MODEL),
        "kv_a_norm_g": g((KV_RANK,)),
        "w_ukv": w((KV_RANK, N_HEADS * (QK_NOPE + V_DIM)), KV_RANK),
        "mla_w_dq": w((N_B_LAYERS, D_MODEL, Q_RANK), D_MODEL),
        "mla_q_norm_g": g((N_B_LAYERS, Q_RANK)),
        "mla_w_uq": w((N_B_LAYERS, Q_RANK, N_HEADS * (QK_NOPE + QK_ROPE)), Q_RANK),
        "mla_w_o": w((N_B_LAYERS, N_HEADS * V_DIM, D_MODEL), N_HEADS * V_DIM),
    }


def reference(x, positions, ffn_pre_g, ffn_post_g, ffn_w_gate, ffn_w_up, ffn_w_down,
              mix_pre_g, mix_post_g, gmlp_w_in, gmlp_ln_g, gmlp_ln_b, gmlp_w_s, gmlp_b_s,
              gmlp_w_out, kv_norm_g, w_dkv, kv_a_norm_g, w_ukv, mla_w_dq, mla_q_norm_g,
              mla_w_uq, mla_w_o):
    cos, sin = rope_tables(positions)
    h = x
    k_nope = k_rope = v = None
    for layer in range(DEPTH):
        f = swiglu(rms_norm(h, ffn_pre_g[layer, 0]), ffn_w_gate[layer, 0], ffn_w_up[layer, 0], ffn_w_down[layer, 0])
        h = h + 0.5 * rms_norm(f, ffn_post_g[layer, 0])
        n = rms_norm(h, mix_pre_g[layer])
        if layer < N_A_LAYERS:
            m = gmlp_mixer(n, gmlp_w_in[layer], gmlp_ln_g[layer], gmlp_ln_b[layer],
                           gmlp_w_s[layer], gmlp_b_s[layer], gmlp_w_out[layer])
        else:
            j = layer - N_A_LAYERS
            m = mla_mixer(n, k_nope, k_rope, v, mla_w_dq[j], mla_q_norm_g[j], mla_w_uq[j], mla_w_o[j], cos, sin)
        h = h + rms_norm(m, mix_post_g[layer])
        f = swiglu(rms_norm(h, ffn_pre_g[layer, 1]), ffn_w_gate[layer, 1], ffn_w_up[layer, 1], ffn_w_down[layer, 1])
        h = h + 0.5 * rms_norm(f, ffn_post_g[layer, 1])
        if layer == N_A_LAYERS - 1:
            k_nope, k_rope, v = shared_kv(h, kv_norm_g, w_dkv, kv_a_norm_g, w_ukv, cos, sin)
    return h
```

```python
import functools

import jax
import jax.numpy as jnp
from jax import lax
from jax.experimental import pallas as pl
from jax.experimental.pallas import tpu as pltpu

D_MODEL = 1024
D_FF = 2816
CHUNK = 128
GMLP_HALF = 2 * D_MODEL
GMLP_GROUPS = 16
N_HEADS = 8
QK_NOPE = 128
QK_ROPE = 64
V_DIM = 128
KV_RANK = 256
Q_RANK = 512
ROPE_THETA = 10000.0
RMS_EPS = 1e-6
LN_EPS = 1e-5
NEG_INF = -1e30

HEAD_PAD = 256
FF_COLS = 256
TM_FFN = 512
TM_GMLP = 256
TM_PROJ = 512
TQ = 256
VMEM_LIMIT = 56 * 1024 * 1024

BF16 = jnp.bfloat16
F32 = jnp.float32


def _rms(x, g):
    return x * lax.rsqrt(jnp.mean(x * x, axis=-1, keepdims=True) + RMS_EPS) * g


def _const_spec(shape):
    return pl.BlockSpec(shape, lambda *_: (0,) * len(shape), pipeline_mode=pl.Buffered(1))


def _params(n_axes):
    return pltpu.CompilerParams(dimension_semantics=("parallel",) * n_axes,
                                vmem_limit_bytes=VMEM_LIMIT)


def _rope_tab_kernel(pos_ref, invf_ref, tab_t_ref, tab_ref):
    ang = invf_ref[...] * pos_ref[...]
    c = jnp.cos(ang)
    s = jnp.sin(ang)
    t = jnp.concatenate([c, c, -s, s], axis=0)
    tab_t_ref[...] = t
    tab_ref[...] = t.T


def _rope_tables(positions):
    t_total = positions.size
    tm = 2048
    pos = positions.reshape(1, t_total).astype(F32)
    inv_freq = ROPE_THETA ** (-jnp.arange(0, QK_ROPE, 2, dtype=F32) / QK_ROPE)
    return pl.pallas_call(
        _rope_tab_kernel,
        grid=(t_total // tm,),
        in_specs=[pl.BlockSpec((1, tm), lambda i: (0, i)),
                  _const_spec((QK_ROPE // 2, 1))],
        out_specs=[pl.BlockSpec((2 * QK_ROPE, tm), lambda i: (0, i)),
                   pl.BlockSpec((tm, 2 * QK_ROPE), lambda i: (i, 0))],
        out_shape=[jax.ShapeDtypeStruct((2 * QK_ROPE, t_total), F32),
                   jax.ShapeDtypeStruct((t_total, 2 * QK_ROPE), F32)],
        compiler_params=_params(1),
        name="rope_tables",
    )(pos, inv_freq.reshape(QK_ROPE // 2, 1))


def _ffn_kernel(h_ref, pre_g_ref, post_g_ref, wg_ref, wu_ref, wd_ref, o_ref, a_ref):
    h = h_ref[...]
    n = _rms(h, pre_g_ref[...]).astype(BF16)
    for c in range(D_FF // FF_COLS):
        sl = slice(c * FF_COLS, (c + 1) * FF_COLS)
        g = jnp.dot(n, wg_ref[:, sl], preferred_element_type=F32)
        u = jnp.dot(n, wu_ref[:, sl], preferred_element_type=F32)
        a_ref[:, sl] = (g * jax.nn.sigmoid(g) * u).astype(BF16)
    f = jnp.dot(a_ref[...], wd_ref[...], preferred_element_type=F32)
    o_ref[...] = h + 0.5 * _rms(f, post_g_ref[...])


def _ffn(h, pre_g, post_g, wg, wu, wd):
    t_total = h.shape[0]
    tm = TM_FFN
    row = pl.BlockSpec((tm, D_MODEL), lambda i: (i, 0))
    return pl.pallas_call(
        _ffn_kernel,
        grid=(t_total // tm,),
        in_specs=[row, _const_spec((1, D_MODEL)), _const_spec((1, D_MODEL)),
                  _const_spec((D_MODEL, D_FF)), _const_spec((D_MODEL, D_FF)),
                  _const_spec((D_FF, D_MODEL))],
        out_specs=row,
        out_shape=jax.ShapeDtypeStruct((t_total, D_MODEL), F32),
        scratch_shapes=[pltpu.VMEM((tm, D_FF), BF16)],
        compiler_params=_params(1),
        name="ffn",
    )(h, pre_g.reshape(1, D_MODEL), post_g.reshape(1, D_MODEL), wg, wu, wd)


def _gmlp_kernel(h_ref, pre_g_ref, post_g_ref, w_in_ref, ln_g_ref, ln_b_ref, w_s_ref, b_st_ref,
                 w_out_ref, o_ref, u_ref, v_ref, gated_ref):
    tm = h_ref.shape[0]
    h = h_ref[...]
    n = _rms(h, pre_g_ref[...]).astype(BF16)
    cols = 512
    for c in range(GMLP_HALF // cols):
        sl = slice(c * cols, (c + 1) * cols)
        u_ref[:, sl] = jax.nn.gelu(jnp.dot(n, w_in_ref[:, sl], preferred_element_type=F32))
        sv = slice(GMLP_HALF + c * cols, GMLP_HALF + (c + 1) * cols)
        v_ref[:, sl] = jax.nn.gelu(jnp.dot(n, w_in_ref[:, sv], preferred_element_type=F32))
    v = v_ref[...]
    mu = jnp.mean(v, axis=-1, keepdims=True)
    xc = v - mu
    var = jnp.mean(xc * xc, axis=-1, keepdims=True)
    v_ref[...] = xc * lax.rsqrt(var + LN_EPS) * ln_g_ref[...] + ln_b_ref[...]

    t_idx = lax.broadcasted_iota(jnp.int32, (CHUNK, CHUNK), 0)
    c_idx = lax.broadcasted_iota(jnp.int32, (CHUNK, CHUNK), 1)
    causal = c_idx <= t_idx
    for g in range(GMLP_GROUPS):
        gl = slice(g * CHUNK, (g + 1) * CHUNK)
        w = jnp.where(causal, w_s_ref[g], 0.0).astype(BF16)
        bias = b_st_ref[:, g:g + 1]
        for ci in range(tm // CHUNK):
            rows = slice(ci * CHUNK, (ci + 1) * CHUNK)
            sv = jnp.dot(w, v_ref[rows, gl].astype(BF16), preferred_element_type=F32) + bias
            gated_ref[rows, gl] = (u_ref[rows, gl] * sv).astype(BF16)
    m = jnp.dot(gated_ref[...], w_out_ref[...], preferred_element_type=F32)
    o_ref[...] = h + _rms(m, post_g_ref[...])


def _gmlp(h, pre_g, post_g, w_in, ln_g, ln_b, w_s, b_s, w_out):
    t_total = h.shape[0]
    tm = TM_GMLP
    row = pl.BlockSpec((tm, D_MODEL), lambda i: (i, 0))
    return pl.pallas_call(
        _gmlp_kernel,
        grid=(t_total // tm,),
        in_specs=[row, _const_spec((1, D_MODEL)), _const_spec((1, D_MODEL)),
                  _const_spec((D_MODEL, 2 * GMLP_HALF)),
                  _const_spec((1, GMLP_HALF)), _const_spec((1, GMLP_HALF)),
                  _const_spec((GMLP_GROUPS, CHUNK, CHUNK)), _const_spec((CHUNK, GMLP_GROUPS)),
                  _const_spec((GMLP_HALF, D_MODEL))],
        out_specs=row,
        out_shape=jax.ShapeDtypeStruct((t_total, D_MODEL), F32),
        scratch_shapes=[pltpu.VMEM((tm, GMLP_HALF), F32), pltpu.VMEM((tm, GMLP_HALF), F32),
                        pltpu.VMEM((tm, GMLP_HALF), BF16)],
        compiler_params=_params(1),
        name="gmlp",
    )(h, pre_g.reshape(1, D_MODEL), post_g.reshape(1, D_MODEL), w_in,
      ln_g.reshape(1, GMLP_HALF), ln_b.reshape(1, GMLP_HALF), w_s, b_s.T, w_out)


def _kv_kernel(h_ref, g_ref, w_c_ref, w_rope_t_ref, ga_ref, w_uk_t_ref, w_uv_ref, tab_t_ref,
               kt_ref, v_ref):
    n = _rms(h_ref[...], g_ref[...]).astype(BF16)
    c = jnp.dot(n, w_c_ref[...], preferred_element_type=F32)
    c = _rms(c, ga_ref[...]).astype(BF16)
    nt = (((1,), (1,)), ((), ()))
    k_t = lax.dot_general(w_uk_t_ref[...], c, nt, preferred_element_type=F32)
    v_ref[...] = jnp.dot(c, w_uv_ref[...], preferred_element_type=F32).astype(BF16)
    pre = lax.dot_general(w_rope_t_ref[...], n, nt, preferred_element_type=F32)
    prod = pre * tab_t_ref[...]
    kr_t = (prod[:QK_ROPE] + prod[QK_ROPE:]).astype(BF16)
    pad = jnp.zeros((HEAD_PAD - QK_NOPE - QK_ROPE, kr_t.shape[1]), BF16)
    for hh in range(N_HEADS):
        kt_ref[hh, 0:QK_NOPE, :] = k_t[hh * QK_NOPE:(hh + 1) * QK_NOPE].astype(BF16)
        kt_ref[hh, QK_NOPE:QK_NOPE + QK_ROPE, :] = kr_t
        kt_ref[hh, QK_NOPE + QK_ROPE:, :] = pad


def _shared_kv(h, batch, seq, g, w_c, w_rope_t, ga, w_uk_t, w_uv, tab_t):
    tm = TM_PROJ
    per_b = seq // tm
    return pl.pallas_call(
        _kv_kernel,
        grid=(batch, per_b),
        in_specs=[pl.BlockSpec((tm, D_MODEL), lambda b, j: (b * per_b + j, 0)),
                  _const_spec((1, D_MODEL)), _const_spec((D_MODEL, KV_RANK)),
                  _const_spec((2 * QK_ROPE, D_MODEL)), _const_spec((1, KV_RANK)),
                  _const_spec((N_HEADS * QK_NOPE, KV_RANK)),
                  _const_spec((KV_RANK, N_HEADS * V_DIM)),
                  pl.BlockSpec((2 * QK_ROPE, tm), lambda b, j: (0, b * per_b + j))],
        out_specs=[pl.BlockSpec((None, N_HEADS, HEAD_PAD, tm), lambda b, j: (b, 0, 0, j)),
                   pl.BlockSpec((tm, N_HEADS * V_DIM), lambda b, j: (b * per_b + j, 0))],
        out_shape=[jax.ShapeDtypeStruct((batch, N_HEADS, HEAD_PAD, seq), BF16),
                   jax.ShapeDtypeStruct((batch * seq, N_HEADS * V_DIM), BF16)],
        compiler_params=_params(2),
        name="shared_kv",
    )(h, g.reshape(1, D_MODEL), w_c, w_rope_t, ga.reshape(1, KV_RANK), w_uk_t, w_uv, tab_t)


def _q_kernel(h_ref, g_ref, w_dq_ref, gq_ref, w_uq_ref, tab_ref, q_ref):
    n = _rms(h_ref[...], g_ref[...]).astype(BF16)
    qd = jnp.dot(n, w_dq_ref[...], preferred_element_type=F32)
    qn = _rms(qd, gq_ref[...]).astype(BF16)
    tab = tab_ref[...]
    for hh in range(N_HEADS):
        base = hh * HEAD_PAD
        q = jnp.dot(qn, w_uq_ref[:, base:base + HEAD_PAD], preferred_element_type=F32)
        q_ref[:, base:base + QK_NOPE] = q[:, :QK_NOPE].astype(BF16)
        prod = q[:, QK_NOPE:] * tab
        rope = prod + pltpu.roll(prod, QK_ROPE, axis=1)
        q_ref[:, base + QK_NOPE:base + HEAD_PAD] = rope.astype(BF16)


def _mla_q(h, g, w_dq, gq, w_uq, tab):
    t_total = h.shape[0]
    tm = TM_PROJ
    return pl.pallas_call(
        _q_kernel,
        grid=(t_total // tm,),
        in_specs=[pl.BlockSpec((tm, D_MODEL), lambda i: (i, 0)),
                  _const_spec((1, D_MODEL)), _const_spec((D_MODEL, Q_RANK)),
                  _const_spec((1, Q_RANK)), _const_spec((Q_RANK, N_HEADS * HEAD_PAD)),
                  pl.BlockSpec((tm, 2 * QK_ROPE), lambda i: (i, 0))],
        out_specs=pl.BlockSpec((tm, N_HEADS * HEAD_PAD), lambda i: (i, 0)),
        out_shape=jax.ShapeDtypeStruct((t_total, N_HEADS * HEAD_PAD), BF16),
        compiler_params=_params(1),
        name="mla_q",
    )(h, g.reshape(1, D_MODEL), w_dq, gq.reshape(1, Q_RANK), w_uq, tab)


def _attn_kernel(q_ref, kt_ref, v_ref, o_ref):
    seq = q_ref.shape[0]
    scale = (QK_NOPE + QK_ROPE) ** -0.5
    r_idx = lax.broadcasted_iota(jnp.int32, (TQ, TQ), 0)
    c_idx = lax.broadcasted_iota(jnp.int32, (TQ, TQ), 1)
    causal = c_idx <= r_idx
    for qi in range(seq // TQ):
        q0, q1 = qi * TQ, (qi + 1) * TQ
        q = q_ref[q0:q1, :]
        s_dg = jnp.dot(q, kt_ref[:, q0:q1], preferred_element_type=F32) * scale
        s_dg = jnp.where(causal, s_dg, NEG_INF)
        m = jnp.max(s_dg, axis=-1, keepdims=True)
        if qi > 0:
            s_off = jnp.dot(q, kt_ref[:, :q0], preferred_element_type=F32) * scale
            m = jnp.maximum(m, jnp.max(s_off, axis=-1, keepdims=True))
        p_dg = jnp.exp(s_dg - m)
        l = jnp.sum(p_dg, axis=-1, keepdims=True)
        acc = jnp.dot(p_dg.astype(BF16), v_ref[q0:q1, :], preferred_element_type=F32)
        if qi > 0:
            p_off = jnp.exp(s_off - m)
            l = l + jnp.sum(p_off, axis=-1, keepdims=True)
            acc = acc + jnp.dot(p_off.astype(BF16), v_ref[:q0, :], preferred_element_type=F32)
        o_ref[q0:q1, :] = (acc / l).astype(BF16)


def _attention(q, kt, v, batch, seq):
    q3 = q.reshape(batch, seq, N_HEADS * HEAD_PAD)
    v3 = v.reshape(batch, seq, N_HEADS * V_DIM)
    out = pl.pallas_call(
        _attn_kernel,
        grid=(batch, N_HEADS),
        in_specs=[pl.BlockSpec((None, seq, HEAD_PAD), lambda b, h: (b, 0, h)),
                  pl.BlockSpec((None, None, HEAD_PAD, seq), lambda b, h: (b, h, 0, 0)),
                  pl.BlockSpec((None, seq, V_DIM), lambda b, h: (b, 0, h))],
        out_specs=pl.BlockSpec((None, seq, V_DIM), lambda b, h: (b, 0, h)),
        out_shape=jax.ShapeDtypeStruct((batch, seq, N_HEADS * V_DIM), BF16),
        compiler_params=_params(2),
        name="mla_attention",
    )(q3, kt, v3)
    return out.reshape(batch * seq, N_HEADS * V_DIM)


def _oproj_kernel(h_ref, o_ref, w_ref, g_ref, out_ref):
    m = jnp.dot(o_ref[...], w_ref[...], preferred_element_type=F32)
    out_ref[...] = h_ref[...] + _rms(m, g_ref[...])


def _oproj(h, o, w_o, g):
    t_total = h.shape[0]
    tm = TM_PROJ
    row = pl.BlockSpec((tm, D_MODEL), lambda i: (i, 0))
    return pl.pallas_call(
        _oproj_kernel,
        grid=(t_total // tm,),
        in_specs=[row, pl.BlockSpec((tm, N_HEADS * V_DIM), lambda i: (i, 0)),
                  _const_spec((N_HEADS * V_DIM, D_MODEL)), _const_spec((1, D_MODEL))],
        out_specs=row,
        out_shape=jax.ShapeDtypeStruct((t_total, D_MODEL), F32),
        compiler_params=_params(1),
        name="mla_oproj",
    )(h, o, w_o, g.reshape(1, D_MODEL))


def _swap_halves(w):
    half = w.shape[-1] // 2
    return jnp.concatenate([w[..., half:], w[..., :half]], axis=-1)


def _kv_weights(w_dkv, w_ukv):
    w_c = w_dkv[:, :KV_RANK].astype(BF16)
    w_r = w_dkv[:, KV_RANK:]
    w_rope_t = jnp.concatenate([w_r, _swap_halves(w_r)], axis=1).T.astype(BF16)
    w4 = w_ukv.reshape(KV_RANK, N_HEADS, QK_NOPE + V_DIM)
    w_uk_t = w4[:, :, :QK_NOPE].reshape(KV_RANK, N_HEADS * QK_NOPE).T.astype(BF16)
    w_uv = w4[:, :, QK_NOPE:].reshape(KV_RANK, N_HEADS * V_DIM).astype(BF16)
    return w_c, w_rope_t, w_uk_t, w_uv


def _q_weights(w_uq):
    w4 = w_uq.reshape(Q_RANK, N_HEADS, QK_NOPE + QK_ROPE)
    w_r = w4[:, :, QK_NOPE:]
    w = jnp.concatenate([w4[:, :, :QK_NOPE], w_r, _swap_halves(w_r)], axis=-1)
    return w.reshape(Q_RANK, N_HEADS * HEAD_PAD).astype(BF16)


def kernel(x, positions, ffn_pre_g, ffn_post_g, ffn_w_gate, ffn_w_up, ffn_w_down, mix_pre_g, mix_post_g, gmlp_w_in, gmlp_ln_g, gmlp_ln_b, gmlp_w_s, gmlp_b_s, gmlp_w_out, kv_norm_g, w_dkv, kv_a_norm_g, w_ukv, mla_w_dq, mla_q_norm_g, mla_w_uq, mla_w_o):
    batch, seq, _ = x.shape
    h = x.reshape(batch * seq, D_MODEL)
    tab_t, tab = _rope_tables(positions)

    def ffn(h, layer, j):
        return _ffn(h, ffn_pre_g[layer, j], ffn_post_g[layer, j],
                    ffn_w_gate[layer, j].astype(BF16), ffn_w_up[layer, j].astype(BF16),
                    ffn_w_down[layer, j].astype(BF16))

    h = ffn(h, 0, 0)
    h = _gmlp(h, mix_pre_g[0], mix_post_g[0], gmlp_w_in[0].astype(BF16), gmlp_ln_g[0],
              gmlp_ln_b[0], gmlp_w_s[0], gmlp_b_s[0], gmlp_w_out[0].astype(BF16))
    h = ffn(h, 0, 1)
    w_c, w_rope_t, w_uk_t, w_uv = _kv_weights(w_dkv, w_ukv)
    kt, v = _shared_kv(h, batch, seq, kv_norm_g, w_c, w_rope_t, kv_a_norm_g, w_uk_t, w_uv, tab_t)
    h = ffn(h, 1, 0)
    q = _mla_q(h, mix_pre_g[1], mla_w_dq[0].astype(BF16), mla_q_norm_g[0],
               _q_weights(mla_w_uq[0]), tab)
    o = _attention(q, kt, v, batch, seq)
    h = _oproj(h, o, mla_w_o[0].astype(BF16), mix_post_g[1])
    h = ffn(h, 1, 1)
    return h.reshape(batch, seq, D_MODEL)
```

```python
import jax
import jax.numpy as jnp
from jax import lax
from jax.experimental import pallas as pl
from jax.experimental.pallas import tpu as pltpu

D_MODEL = 1024
D_FF = 2816
CHUNK = 128
GMLP_HALF = 2 * D_MODEL
GMLP_GROUPS = 16
N_HEADS = 8
QK_NOPE = 128
QK_ROPE = 64
V_DIM = 128
KV_RANK = 256
Q_RANK = 512
ROPE_THETA = 10000.0
RMS_EPS = 1e-6
LN_EPS = 1e-5
NEG_INF = -1e30

HEAD_PAD = 256
FF_COLS = 256
TM_FFN = 1024
SUB_FFN = 512
TM_GMLP = 512
SUB_GMLP = 256
TM_PROJ = 512
SUB_PROJ = 256
TQ = 256
VMEM_LIMIT = 56 * 1024 * 1024

BF16 = jnp.bfloat16
F32 = jnp.float32


def _rms(x, g):
    return x * lax.rsqrt(jnp.mean(x * x, axis=-1, keepdims=True) + RMS_EPS) * g


def _const_spec(shape):
    return pl.BlockSpec(shape, lambda *_: (0,) * len(shape), pipeline_mode=pl.Buffered(1))


def _params(n_axes):
    return pltpu.CompilerParams(dimension_semantics=("parallel",) * n_axes,
                                vmem_limit_bytes=VMEM_LIMIT)


def _rope_tab_kernel(pos_ref, invf_ref, tab_ref):
    ang = invf_ref[...] * pos_ref[...]
    c = jnp.cos(ang)
    s = jnp.sin(ang)
    tab_ref[...] = jnp.concatenate([c, c, -s, s], axis=0).T


def _rope_table(positions):
    t_total = positions.size
    tm = 2048
    pos = positions.reshape(1, t_total).astype(F32)
    inv_freq = ROPE_THETA ** (-jnp.arange(0, QK_ROPE, 2, dtype=F32) / QK_ROPE)
    return pl.pallas_call(
        _rope_tab_kernel,
        grid=(t_total // tm,),
        in_specs=[pl.BlockSpec((1, tm), lambda i: (0, i)),
                  _const_spec((QK_ROPE // 2, 1))],
        out_specs=pl.BlockSpec((tm, 2 * QK_ROPE), lambda i: (i, 0)),
        out_shape=jax.ShapeDtypeStruct((t_total, 2 * QK_ROPE), F32),
        compiler_params=_params(1),
        name="rope_table",
    )(pos, inv_freq.reshape(QK_ROPE // 2, 1))


def _rope(x2, tab):
    prod = x2 * tab
    return prod + pltpu.roll(prod, QK_ROPE, axis=1)


def _ffn_kernel(h_ref, pre_g_ref, post_g_ref, wg_ref, wu_ref, wd_ref, o_ref, a_ref):
    n_sub = h_ref.shape[0] // SUB_FFN

    def rows(i):
        return slice(i * SUB_FFN, (i + 1) * SUB_FFN)

    for i in range(n_sub):
        n = _rms(h_ref[rows(i), :], pre_g_ref[...]).astype(BF16)
        for c in range(D_FF // FF_COLS):
            sl = slice(c * FF_COLS, (c + 1) * FF_COLS)
            g = jnp.dot(n, wg_ref[:, sl], preferred_element_type=F32)
            u = jnp.dot(n, wu_ref[:, sl], preferred_element_type=F32)
            a_ref[rows(i), sl] = (g * jax.nn.sigmoid(g) * u).astype(BF16)
    for i in range(n_sub):
        f = jnp.dot(a_ref[rows(i), :], wd_ref[...], preferred_element_type=F32)
        o_ref[rows(i), :] = h_ref[rows(i), :] + 0.5 * _rms(f, post_g_ref[...])


def _ffn(h, pre_g, post_g, wg, wu, wd):
    t_total = h.shape[0]
    tm = TM_FFN
    row = pl.BlockSpec((tm, D_MODEL), lambda i: (i, 0))
    return pl.pallas_call(
        _ffn_kernel,
        grid=(t_total // tm,),
        in_specs=[row, _const_spec((1, D_MODEL)), _const_spec((1, D_MODEL)),
                  _const_spec((D_MODEL, D_FF)), _const_spec((D_MODEL, D_FF)),
                  _const_spec((D_FF, D_MODEL))],
        out_specs=row,
        out_shape=jax.ShapeDtypeStruct((t_total, D_MODEL), F32),
        scratch_shapes=[pltpu.VMEM((tm, D_FF), BF16)],
        compiler_params=_params(1),
        name="ffn",
    )(h, pre_g.reshape(1, D_MODEL), post_g.reshape(1, D_MODEL), wg, wu, wd)


def _gmlp_kernel(h_ref, pre_g_ref, post_g_ref, w_in_ref, ln_g_ref, ln_b_ref, w_s_ref, b_st_ref,
                 w_out_ref, o_ref, wm_ref, bias_ref, v_ref, vn_ref, gated_ref):
    n_sub = h_ref.shape[0] // SUB_GMLP

    @pl.when(pl.program_id(0) == 0)
    def _():
        t_idx = lax.broadcasted_iota(jnp.int32, (CHUNK, CHUNK), 0)
        c_idx = lax.broadcasted_iota(jnp.int32, (CHUNK, CHUNK), 1)
        for g in range(GMLP_GROUPS):
            wm_ref[g] = jnp.where(c_idx <= t_idx, w_s_ref[g], 0.0).astype(BF16)
            bias_ref[g] = jnp.broadcast_to(b_st_ref[:, g:g + 1], (CHUNK, CHUNK))

    def rows(i):
        return slice(i * SUB_GMLP, (i + 1) * SUB_GMLP)

    n = [_rms(h_ref[rows(i), :], pre_g_ref[...]).astype(BF16) for i in range(n_sub)]
    cols = 512
    for i in range(n_sub):
        for c in range(GMLP_HALF // cols):
            z = jnp.dot(n[i], w_in_ref[:, GMLP_HALF + c * cols:GMLP_HALF + (c + 1) * cols],
                        preferred_element_type=F32)
            v_ref[rows(i), c * cols:(c + 1) * cols] = jax.nn.gelu(z)
    for i in range(n_sub):
        v = v_ref[rows(i), :]
        mu = jnp.mean(v, axis=-1, keepdims=True)
        xc = v - mu
        var = jnp.mean(xc * xc, axis=-1, keepdims=True)
        vn = xc * lax.rsqrt(var + LN_EPS) * ln_g_ref[...] + ln_b_ref[...]
        vn_ref[rows(i), :] = vn.astype(BF16)
    for i in range(n_sub):
        for gp in range(GMLP_GROUPS // 2):
            u = jax.nn.gelu(jnp.dot(n[i], w_in_ref[:, gp * 2 * CHUNK:(gp + 1) * 2 * CHUNK],
                                    preferred_element_type=F32))
            for gg in range(2):
                g = 2 * gp + gg
                gl = slice(g * CHUNK, (g + 1) * CHUNK)
                for ci in range(SUB_GMLP // CHUNK):
                    r = slice(i * SUB_GMLP + ci * CHUNK, i * SUB_GMLP + (ci + 1) * CHUNK)
                    sv = jnp.dot(wm_ref[g], vn_ref[r, gl], preferred_element_type=F32) + bias_ref[g]
                    u_blk = u[ci * CHUNK:(ci + 1) * CHUNK, gg * CHUNK:(gg + 1) * CHUNK]
                    gated_ref[r, gl] = (u_blk * sv).astype(BF16)
    for i in range(n_sub):
        m = jnp.dot(gated_ref[rows(i), :], w_out_ref[...], preferred_element_type=F32)
        o_ref[rows(i), :] = h_ref[rows(i), :] + _rms(m, post_g_ref[...])


def _gmlp(h, pre_g, post_g, w_in, ln_g, ln_b, w_s, b_s, w_out):
    t_total = h.shape[0]
    tm = TM_GMLP
    row = pl.BlockSpec((tm, D_MODEL), lambda i: (i, 0))
    return pl.pallas_call(
        _gmlp_kernel,
        grid=(t_total // tm,),
        in_specs=[row, _const_spec((1, D_MODEL)), _const_spec((1, D_MODEL)),
                  _const_spec((D_MODEL, 2 * GMLP_HALF)),
                  _const_spec((1, GMLP_HALF)), _const_spec((1, GMLP_HALF)),
                  _const_spec((GMLP_GROUPS, CHUNK, CHUNK)), _const_spec((CHUNK, GMLP_GROUPS)),
                  _const_spec((GMLP_HALF, D_MODEL))],
        out_specs=row,
        out_shape=jax.ShapeDtypeStruct((t_total, D_MODEL), F32),
        scratch_shapes=[pltpu.VMEM((GMLP_GROUPS, CHUNK, CHUNK), BF16),
                        pltpu.VMEM((GMLP_GROUPS, CHUNK, CHUNK), F32),
                        pltpu.VMEM((tm, GMLP_HALF), F32), pltpu.VMEM((tm, GMLP_HALF), BF16),
                        pltpu.VMEM((tm, GMLP_HALF), BF16)],
        compiler_params=pltpu.CompilerParams(dimension_semantics=("arbitrary",),
                                             vmem_limit_bytes=VMEM_LIMIT),
        name="gmlp",
    )(h, pre_g.reshape(1, D_MODEL), post_g.reshape(1, D_MODEL), w_in,
      ln_g.reshape(1, GMLP_HALF), ln_b.reshape(1, GMLP_HALF), w_s, b_s.T, w_out)


def _sub_rows(i):
    return slice(i * SUB_PROJ, (i + 1) * SUB_PROJ)


def _kv_kernel(h_ref, g_ref, w_c_ref, w_rope_ref, ga_ref, w_uk_t_ref, w_uv_ref, tab_ref,
               knt_ref, krt_ref, v_ref):
    n_sub = h_ref.shape[0] // SUB_PROJ
    nt = (((1,), (1,)), ((), ()))
    ns = [_rms(h_ref[_sub_rows(i), :], g_ref[...]).astype(BF16) for i in range(n_sub)]
    cs = [_rms(jnp.dot(n, w_c_ref[...], preferred_element_type=F32), ga_ref[...]).astype(BF16)
          for n in ns]
    for i in range(n_sub):
        r = _sub_rows(i)
        k_t = lax.dot_general(w_uk_t_ref[...], cs[i], nt, preferred_element_type=F32)
        knt_ref[:, r] = k_t.astype(BF16)
        v_ref[r, :] = jnp.dot(cs[i], w_uv_ref[...], preferred_element_type=F32).astype(BF16)
        k2 = jnp.dot(ns[i], w_rope_ref[...], preferred_element_type=F32)
        krt_ref[:, r] = _rope(k2, tab_ref[r, :]).T[:QK_ROPE].astype(BF16)


def _shared_kv(h, batch, seq, g, w_c, w_rope, ga, w_uk_t, w_uv, tab):
    tm = TM_PROJ
    per_b = seq // tm
    return pl.pallas_call(
        _kv_kernel,
        grid=(batch, per_b),
        in_specs=[pl.BlockSpec((tm, D_MODEL), lambda b, j: (b * per_b + j, 0)),
                  _const_spec((1, D_MODEL)), _const_spec((D_MODEL, KV_RANK)),
                  _const_spec((D_MODEL, 2 * QK_ROPE)), _const_spec((1, KV_RANK)),
                  _const_spec((N_HEADS * QK_NOPE, KV_RANK)),
                  _const_spec((KV_RANK, N_HEADS * V_DIM)),
                  pl.BlockSpec((tm, 2 * QK_ROPE), lambda b, j: (b * per_b + j, 0))],
        out_specs=[pl.BlockSpec((None, N_HEADS * QK_NOPE, tm), lambda b, j: (b, 0, j)),
                   pl.BlockSpec((None, QK_ROPE, tm), lambda b, j: (b, 0, j)),
                   pl.BlockSpec((tm, N_HEADS * V_DIM), lambda b, j: (b * per_b + j, 0))],
        out_shape=[jax.ShapeDtypeStruct((batch, N_HEADS * QK_NOPE, seq), BF16),
                   jax.ShapeDtypeStruct((batch, QK_ROPE, seq), BF16),
                   jax.ShapeDtypeStruct((batch * seq, N_HEADS * V_DIM), BF16)],
        compiler_params=_params(2),
        name="shared_kv",
    )(h, g.reshape(1, D_MODEL), w_c, w_rope, ga.reshape(1, KV_RANK), w_uk_t, w_uv, tab)


def _q_kernel(h_ref, g_ref, w_dq_ref, gq_ref, w_uq_ref, tab_ref, q_ref):
    n_sub = h_ref.shape[0] // SUB_PROJ
    ns = [_rms(h_ref[_sub_rows(i), :], g_ref[...]).astype(BF16) for i in range(n_sub)]
    qns = [_rms(jnp.dot(n, w_dq_ref[...], preferred_element_type=F32), gq_ref[...]).astype(BF16)
           for n in ns]
    for i in range(n_sub):
        r = _sub_rows(i)
        tab = tab_ref[r, :]
        for hh in range(N_HEADS):
            base = hh * HEAD_PAD
            q = jnp.dot(qns[i], w_uq_ref[:, base:base + HEAD_PAD], preferred_element_type=F32)
            q_ref[r, base:base + QK_NOPE] = q[:, :QK_NOPE].astype(BF16)
            q_ref[r, base + QK_NOPE:base + HEAD_PAD] = _rope(q[:, QK_NOPE:], tab).astype(BF16)


def _mla_q(h, g, w_dq, gq, w_uq, tab):
    t_total = h.shape[0]
    tm = TM_PROJ
    return pl.pallas_call(
        _q_kernel,
        grid=(t_total // tm,),
        in_specs=[pl.BlockSpec((tm, D_MODEL), lambda i: (i, 0)),
                  _const_spec((1, D_MODEL)), _const_spec((D_MODEL, Q_RANK)),
                  _const_spec((1, Q_RANK)), _const_spec((Q_RANK, N_HEADS * HEAD_PAD)),
                  pl.BlockSpec((tm, 2 * QK_ROPE), lambda i: (i, 0))],
        out_specs=pl.BlockSpec((tm, N_HEADS * HEAD_PAD), lambda i: (i, 0)),
        out_shape=jax.ShapeDtypeStruct((t_total, N_HEADS * HEAD_PAD), BF16),
        compiler_params=_params(1),
        name="mla_q",
    )(h, g.reshape(1, D_MODEL), w_dq, gq.reshape(1, Q_RANK), w_uq, tab)


def _attn_kernel(q_ref, knt_ref, krt_ref, v_ref, o_ref, kt_ref, va_ref):
    seq = q_ref.shape[0]
    n_blk = seq // TQ
    c = (QK_NOPE + QK_ROPE) ** -0.5 * 1.4426950408889634
    kt_ref[:QK_NOPE, :] = knt_ref[...]
    kt_ref[QK_NOPE:QK_NOPE + QK_ROPE, :] = krt_ref[...]
    kt_ref[QK_NOPE + QK_ROPE:, :] = jnp.zeros((HEAD_PAD - QK_NOPE - QK_ROPE, seq), BF16)
    va_ref[:, :V_DIM] = v_ref[...]
    va_ref[:, V_DIM:] = jnp.ones((seq, V_DIM), BF16)
    r_idx = lax.broadcasted_iota(jnp.int32, (TQ, TQ), 0)
    c_idx = lax.broadcasted_iota(jnp.int32, (TQ, TQ), 1)
    causal = c_idx <= r_idx

    def scores(qi):
        return jnp.dot(q_ref[qi * TQ:(qi + 1) * TQ, :], kt_ref[:, :(qi + 1) * TQ],
                       preferred_element_type=F32)

    s_next = scores(0)
    for qi in range(n_blk):
        q0, q1 = qi * TQ, (qi + 1) * TQ
        s = s_next
        if qi + 1 < n_blk:
            s_next = scores(qi + 1)
        s_dg = jnp.where(causal, s[:, q0:q1], NEG_INF)
        m = jnp.max(s_dg, axis=-1, keepdims=True)
        if qi > 0:
            m = jnp.maximum(m, jnp.max(s[:, :q0], axis=-1, keepdims=True))
        p = jnp.exp2((s_dg - m) * c).astype(BF16)
        if qi > 0:
            p = jnp.concatenate([jnp.exp2((s[:, :q0] - m) * c).astype(BF16), p], axis=1)
        acc = jnp.dot(p, va_ref[:q1, :], preferred_element_type=F32)
        o_ref[q0:q1, :] = (acc[:, :V_DIM] / acc[:, V_DIM:]).astype(BF16)


def _attention(q, knt, krt, v, batch, seq):
    q3 = q.reshape(batch, seq, N_HEADS * HEAD_PAD)
    v3 = v.reshape(batch, seq, N_HEADS * V_DIM)
    out = pl.pallas_call(
        _attn_kernel,
        grid=(batch, N_HEADS),
        in_specs=[pl.BlockSpec((None, seq, HEAD_PAD), lambda b, h: (b, 0, h)),
                  pl.BlockSpec((None, QK_NOPE, seq), lambda b, h: (b, h, 0)),
                  pl.BlockSpec((None, QK_ROPE, seq), lambda b, h: (b, 0, 0)),
                  pl.BlockSpec((None, seq, V_DIM), lambda b, h: (b, 0, h))],
        out_specs=pl.BlockSpec((None, seq, V_DIM), lambda b, h: (b, 0, h)),
        out_shape=jax.ShapeDtypeStruct((batch, seq, N_HEADS * V_DIM), BF16),
        scratch_shapes=[pltpu.VMEM((HEAD_PAD, seq), BF16), pltpu.VMEM((seq, 2 * V_DIM), BF16)],
        compiler_params=_params(2),
        name="mla_attention",
    )(q3, knt, krt, v3)
    return out.reshape(batch * seq, N_HEADS * V_DIM)


def _oproj_kernel(h_ref, o_ref, w_ref, g_ref, out_ref):
    m = jnp.dot(o_ref[...], w_ref[...], preferred_element_type=F32)
    out_ref[...] = h_ref[...] + _rms(m, g_ref[...])


def _oproj(h, o, w_o, g):
    t_total = h.shape[0]
    tm = TM_PROJ
    row = pl.BlockSpec((tm, D_MODEL), lambda i: (i, 0))
    return pl.pallas_call(
        _oproj_kernel,
        grid=(t_total // tm,),
        in_specs=[row, pl.BlockSpec((tm, N_HEADS * V_DIM), lambda i: (i, 0)),
                  _const_spec((N_HEADS * V_DIM, D_MODEL)), _const_spec((1, D_MODEL))],
        out_specs=row,
        out_shape=jax.ShapeDtypeStruct((t_total, D_MODEL), F32),
        compiler_params=_params(1),
        name="mla_oproj",
    )(h, o, w_o, g.reshape(1, D_MODEL))


def _swap_halves(w):
    half = w.shape[-1] // 2
    return jnp.concatenate([w[..., half:], w[..., :half]], axis=-1)


def _kv_weights(w_dkv, w_ukv):
    w_c = w_dkv[:, :KV_RANK].astype(BF16)
    w_r = w_dkv[:, KV_RANK:]
    w_rope = jnp.concatenate([w_r, _swap_halves(w_r)], axis=1).astype(BF16)
    w4 = w_ukv.reshape(KV_RANK, N_HEADS, QK_NOPE + V_DIM)
    w_uk_t = w4[:, :, :QK_NOPE].reshape(KV_RANK, N_HEADS * QK_NOPE).T.astype(BF16)
    w_uv = w4[:, :, QK_NOPE:].reshape(KV_RANK, N_HEADS * V_DIM).astype(BF16)
    return w_c, w_rope, w_uk_t, w_uv


def _q_weights(w_uq):
    w4 = w_uq.reshape(Q_RANK, N_HEADS, QK_NOPE + QK_ROPE)
    w_r = w4[:, :, QK_NOPE:]
    w = jnp.concatenate([w4[:, :, :QK_NOPE], w_r, _swap_halves(w_r)], axis=-1)
    return w.reshape(Q_RANK, N_HEADS * HEAD_PAD).astype(BF16)


def kernel(x, positions, ffn_pre_g, ffn_post_g, ffn_w_gate, ffn_w_up, ffn_w_down, mix_pre_g, mix_post_g, gmlp_w_in, gmlp_ln_g, gmlp_ln_b, gmlp_w_s, gmlp_b_s, gmlp_w_out, kv_norm_g, w_dkv, kv_a_norm_g, w_ukv, mla_w_dq, mla_q_norm_g, mla_w_uq, mla_w_o):
    batch, seq, _ = x.shape
    h = x.reshape(batch * seq, D_MODEL)
    tab = _rope_table(positions)

    def ffn(h, layer, j):
        return _ffn(h, ffn_pre_g[layer, j], ffn_post_g[layer, j],
                    ffn_w_gate[layer, j].astype(BF16), ffn_w_up[layer, j].astype(BF16),
                    ffn_w_down[layer, j].astype(BF16))

    h = ffn(h, 0, 0)
    h = _gmlp(h, mix_pre_g[0], mix_post_g[0], gmlp_w_in[0].astype(BF16), gmlp_ln_g[0],
              gmlp_ln_b[0], gmlp_w_s[0], gmlp_b_s[0], gmlp_w_out[0].astype(BF16))
    h = ffn(h, 0, 1)
    w_c, w_rope, w_uk_t, w_uv = _kv_weights(w_dkv, w_ukv)
    knt, krt, v = _shared_kv(h, batch, seq, kv_norm_g, w_c, w_rope, kv_a_norm_g, w_uk_t, w_uv, tab)
    h = ffn(h, 1, 0)
    q = _mla_q(h, mix_pre_g[1], mla_w_dq[0].astype(BF16), mla_q_norm_g[0],
               _q_weights(mla_w_uq[0]), tab)
    o = _attention(q, knt, krt, v, batch, seq)
    h = _oproj(h, o, mla_w_o[0].astype(BF16), mix_post_g[1])
    h = ffn(h, 1, 1)
    return h.reshape(batch, seq, D_MODEL)
```

```python
import jax
import jax.numpy as jnp
from jax import lax
from jax.experimental import pallas as pl
from jax.experimental.pallas import tpu as pltpu

D_MODEL = 1024
D_FF = 2816
CHUNK = 128
GMLP_HALF = 2 * D_MODEL
GMLP_GROUPS = 16
N_HEADS = 8
QK_NOPE = 128
QK_ROPE = 64
V_DIM = 128
KV_RANK = 256
Q_RANK = 512
ROPE_THETA = 10000.0
RMS_EPS = 1e-6
LN_EPS = 1e-5
NEG_INF = -1e30
QK_SCALE_LOG2E = (QK_NOPE + QK_ROPE) ** -0.5 * 1.4426950408889634

HEAD_PAD = 256
FF_COLS = 256
TM_FFN = 1024
SUB_FFN = 512
TM_GMLP = 1024
SUB_GMLP = 256
TM_FUSED = 512
SUB_FUSED = 256
TQ = 256
HEADS_PER_STEP = 2
VMEM_LIMIT = 56 * 1024 * 1024

BF16 = jnp.bfloat16
F32 = jnp.float32


def _rms(x, g):
    return x * lax.rsqrt(jnp.mean(x * x, axis=-1, keepdims=True) + RMS_EPS) * g


def _const_spec(shape):
    return pl.BlockSpec(shape, lambda *_: (0,) * len(shape), pipeline_mode=pl.Buffered(1))


def _row_spec(tm, width):
    return pl.BlockSpec((tm, width), lambda i: (i, 0))


def _row_tiles(rows, sub):
    return [slice(i * sub, (i + 1) * sub) for i in range(rows // sub)]


def _params(n_axes):
    return pltpu.CompilerParams(dimension_semantics=("parallel",) * n_axes,
                                vmem_limit_bytes=VMEM_LIMIT)


def _rope_tab_kernel(pos_ref, invf_ref, tab_ref):
    ang = invf_ref[...] * pos_ref[...]
    c = jnp.cos(ang)
    s = jnp.sin(ang)
    tab_ref[...] = jnp.concatenate([c, c, -s, s], axis=0).T


def _rope_table(positions):
    t_total = positions.size
    tm = 2048
    pos = positions.reshape(1, t_total).astype(F32)
    inv_freq = ROPE_THETA ** (-jnp.arange(0, QK_ROPE, 2, dtype=F32) / QK_ROPE)
    return pl.pallas_call(
        _rope_tab_kernel,
        grid=(t_total // tm,),
        in_specs=[pl.BlockSpec((1, tm), lambda i: (0, i)),
                  _const_spec((QK_ROPE // 2, 1))],
        out_specs=pl.BlockSpec((tm, 2 * QK_ROPE), lambda i: (i, 0)),
        out_shape=jax.ShapeDtypeStruct((t_total, 2 * QK_ROPE), F32),
        compiler_params=_params(1),
        name="rope_table",
    )(pos, inv_freq.reshape(QK_ROPE // 2, 1))


def _rope(x2, tab):
    prod = x2 * tab
    return prod + pltpu.roll(prod, QK_ROPE, axis=1)


def _ffn_up(x, pre_g_ref, wg_ref, wu_ref, a_ref, r):
    n = _rms(x, pre_g_ref[...]).astype(BF16)
    for c in range(D_FF // FF_COLS):
        sl = slice(c * FF_COLS, (c + 1) * FF_COLS)
        g = jnp.dot(n, wg_ref[:, sl], preferred_element_type=F32)
        u = jnp.dot(n, wu_ref[:, sl], preferred_element_type=F32)
        a_ref[r, sl] = (g * jax.nn.sigmoid(g) * u).astype(BF16)


def _ffn_down(x, post_g_ref, wd_ref, a_ref, r):
    f = jnp.dot(a_ref[r, :], wd_ref[...], preferred_element_type=F32)
    return x + 0.5 * _rms(f, post_g_ref[...])


def _kv_stage(xs, tiles, g_ref, w_c_ref, w_rope_ref, ga_ref, w_uk_t_ref, w_uv_ref, tab_ref,
              knt_ref, krt_ref, v_ref):
    nt = (((1,), (1,)), ((), ()))
    ns = [_rms(x, g_ref[...]).astype(BF16) for x in xs]
    cs = [_rms(jnp.dot(n, w_c_ref[...], preferred_element_type=F32), ga_ref[...]).astype(BF16)
          for n in ns]
    for n, c, r in zip(ns, cs, tiles):
        k_t = lax.dot_general(w_uk_t_ref[...], c, nt, preferred_element_type=F32)
        knt_ref[:, r] = k_t.astype(BF16)
        v_ref[r, :] = jnp.dot(c, w_uv_ref[...], preferred_element_type=F32).astype(BF16)
        k2 = jnp.dot(n, w_rope_ref[...], preferred_element_type=F32)
        krt_ref[:, r] = _rope(k2, tab_ref[r, :]).T[:QK_ROPE].astype(BF16)


def _q_stage(xs, tiles, g_ref, w_dq_ref, gq_ref, w_uq_ref, tab_ref, q_ref):
    ns = [_rms(x, g_ref[...]).astype(BF16) for x in xs]
    qns = [_rms(jnp.dot(n, w_dq_ref[...], preferred_element_type=F32), gq_ref[...]).astype(BF16)
           for n in ns]
    for qn, r in zip(qns, tiles):
        tab = tab_ref[r, :]
        for hh in range(N_HEADS):
            base = hh * HEAD_PAD
            q = jnp.dot(qn, w_uq_ref[:, base:base + HEAD_PAD], preferred_element_type=F32)
            q = q * QK_SCALE_LOG2E
            q_ref[r, base:base + QK_NOPE] = q[:, :QK_NOPE].astype(BF16)
            q_ref[r, base + QK_NOPE:base + HEAD_PAD] = _rope(q[:, QK_NOPE:], tab).astype(BF16)


def _ffn_kernel(h_ref, pre_g_ref, post_g_ref, wg_ref, wu_ref, wd_ref, o_ref, a_ref):
    tiles = _row_tiles(h_ref.shape[0], SUB_FFN)
    for r in tiles:
        _ffn_up(h_ref[r, :], pre_g_ref, wg_ref, wu_ref, a_ref, r)
    for r in tiles:
        o_ref[r, :] = _ffn_down(h_ref[r, :], post_g_ref, wd_ref, a_ref, r)


def _ffn_kv_kernel(h_ref, pre_g_ref, post_g_ref, wg_ref, wu_ref, wd_ref,
                   g_ref, w_c_ref, w_rope_ref, ga_ref, w_uk_t_ref, w_uv_ref, tab_ref,
                   o_ref, knt_ref, krt_ref, v_ref, a_ref):
    tiles = _row_tiles(h_ref.shape[0], SUB_FUSED)
    for r in tiles:
        _ffn_up(h_ref[r, :], pre_g_ref, wg_ref, wu_ref, a_ref, r)
    for r in tiles:
        o_ref[r, :] = _ffn_down(h_ref[r, :], post_g_ref, wd_ref, a_ref, r)
    _kv_stage([o_ref[r, :] for r in tiles], tiles, g_ref, w_c_ref, w_rope_ref, ga_ref,
              w_uk_t_ref, w_uv_ref, tab_ref, knt_ref, krt_ref, v_ref)


def _ffn_q_kernel(h_ref, pre_g_ref, post_g_ref, wg_ref, wu_ref, wd_ref,
                  g_ref, w_dq_ref, gq_ref, w_uq_ref, tab_ref, o_ref, q_ref, a_ref):
    tiles = _row_tiles(h_ref.shape[0], SUB_FUSED)
    for r in tiles:
        _ffn_up(h_ref[r, :], pre_g_ref, wg_ref, wu_ref, a_ref, r)
    for r in tiles:
        o_ref[r, :] = _ffn_down(h_ref[r, :], post_g_ref, wd_ref, a_ref, r)
    _q_stage([o_ref[r, :] for r in tiles], tiles, g_ref, w_dq_ref, gq_ref, w_uq_ref, tab_ref, q_ref)


def _oproj_ffn_kernel(h_ref, att_ref, w_o_ref, g_mix_ref, pre_g_ref, post_g_ref, wg_ref, wu_ref,
                      wd_ref, o_ref, a_ref, h1_ref):
    tiles = _row_tiles(h_ref.shape[0], SUB_FUSED)
    for r in tiles:
        m = jnp.dot(att_ref[r, :], w_o_ref[...], preferred_element_type=F32)
        h1_ref[r, :] = h_ref[r, :] + _rms(m, g_mix_ref[...])
    for r in tiles:
        _ffn_up(h1_ref[r, :], pre_g_ref, wg_ref, wu_ref, a_ref, r)
    for r in tiles:
        o_ref[r, :] = _ffn_down(h1_ref[r, :], post_g_ref, wd_ref, a_ref, r)


def _ffn_weight_specs(layer, j):
    def stacked(rows, cols):
        return pl.BlockSpec((None, None, rows, cols), lambda *_: (layer, j, 0, 0),
                            pipeline_mode=pl.Buffered(1))
    return [stacked(1, D_MODEL), stacked(1, D_MODEL), stacked(D_MODEL, D_FF),
            stacked(D_MODEL, D_FF), stacked(D_FF, D_MODEL)]


def _ffn(h, ffn_w, layer, j):
    t_total = h.shape[0]
    tm = TM_FFN
    return pl.pallas_call(
        _ffn_kernel,
        grid=(t_total // tm,),
        in_specs=[_row_spec(tm, D_MODEL)] + _ffn_weight_specs(layer, j),
        out_specs=_row_spec(tm, D_MODEL),
        out_shape=jax.ShapeDtypeStruct((t_total, D_MODEL), F32),
        scratch_shapes=[pltpu.VMEM((tm, D_FF), BF16)],
        compiler_params=_params(1),
        name="ffn",
    )(h, *ffn_w)


def _ffn_kv(h, ffn_w, layer, j, batch, seq, g, w_c, w_rope, ga, w_uk_t, w_uv, tab):
    t_total = h.shape[0]
    tm = TM_FUSED
    per_b = seq // tm
    return pl.pallas_call(
        _ffn_kv_kernel,
        grid=(t_total // tm,),
        in_specs=[_row_spec(tm, D_MODEL)] + _ffn_weight_specs(layer, j) + [
            _const_spec((1, D_MODEL)), _const_spec((D_MODEL, KV_RANK)),
            _const_spec((D_MODEL, 2 * QK_ROPE)), _const_spec((1, KV_RANK)),
            _const_spec((N_HEADS * QK_NOPE, KV_RANK)), _const_spec((KV_RANK, N_HEADS * V_DIM)),
            _row_spec(tm, 2 * QK_ROPE)],
        out_specs=[_row_spec(tm, D_MODEL),
                   pl.BlockSpec((None, N_HEADS * QK_NOPE, tm), lambda i: (i // per_b, 0, i % per_b)),
                   pl.BlockSpec((None, QK_ROPE, tm), lambda i: (i // per_b, 0, i % per_b)),
                   _row_spec(tm, N_HEADS * V_DIM)],
        out_shape=[jax.ShapeDtypeStruct((t_total, D_MODEL), F32),
                   jax.ShapeDtypeStruct((batch, N_HEADS * QK_NOPE, seq), BF16),
                   jax.ShapeDtypeStruct((batch, QK_ROPE, seq), BF16),
                   jax.ShapeDtypeStruct((t_total, N_HEADS * V_DIM), BF16)],
        scratch_shapes=[pltpu.VMEM((tm, D_FF), BF16)],
        compiler_params=_params(1),
        name="ffn_kv",
    )(h, *ffn_w, g.reshape(1, D_MODEL), w_c, w_rope, ga.reshape(1, KV_RANK), w_uk_t, w_uv, tab)


def _ffn_q(h, ffn_w, layer, j, g, w_dq, gq, w_uq, tab):
    t_total = h.shape[0]
    tm = TM_FUSED
    return pl.pallas_call(
        _ffn_q_kernel,
        grid=(t_total // tm,),
        in_specs=[_row_spec(tm, D_MODEL)] + _ffn_weight_specs(layer, j) + [
            _const_spec((1, D_MODEL)), _const_spec((D_MODEL, Q_RANK)), _const_spec((1, Q_RANK)),
            _const_spec((Q_RANK, N_HEADS * HEAD_PAD)), _row_spec(tm, 2 * QK_ROPE)],
        out_specs=[_row_spec(tm, D_MODEL), _row_spec(tm, N_HEADS * HEAD_PAD)],
        out_shape=[jax.ShapeDtypeStruct((t_total, D_MODEL), F32),
                   jax.ShapeDtypeStruct((t_total, N_HEADS * HEAD_PAD), BF16)],
        scratch_shapes=[pltpu.VMEM((tm, D_FF), BF16)],
        compiler_params=_params(1),
        name="ffn_q",
    )(h, *ffn_w, g.reshape(1, D_MODEL), w_dq, gq.reshape(1, Q_RANK), w_uq, tab)


def _oproj_ffn(h, att, w_o, g_mix, ffn_w, layer, j):
    t_total = h.shape[0]
    tm = TM_FUSED
    return pl.pallas_call(
        _oproj_ffn_kernel,
        grid=(t_total // tm,),
        in_specs=[_row_spec(tm, D_MODEL), _row_spec(tm, N_HEADS * V_DIM),
                  _const_spec((N_HEADS * V_DIM, D_MODEL)), _const_spec((1, D_MODEL))]
                 + _ffn_weight_specs(layer, j),
        out_specs=_row_spec(tm, D_MODEL),
        out_shape=jax.ShapeDtypeStruct((t_total, D_MODEL), F32),
        scratch_shapes=[pltpu.VMEM((tm, D_FF), BF16), pltpu.VMEM((tm, D_MODEL), F32)],
        compiler_params=_params(1),
        name="oproj_ffn",
    )(h, att, w_o, g_mix.reshape(1, D_MODEL), *ffn_w)


def _gmlp_kernel(h_ref, pre_g_ref, post_g_ref, w_in_ref, ln_g_ref, ln_b_ref, w_s_ref, b_st_ref,
                 w_out_ref, o_ref, wm_ref, bias_ref, v_ref, vn_ref, gated_ref):
    @pl.when(pl.program_id(0) == 0)
    def _():
        t_idx = lax.broadcasted_iota(jnp.int32, (CHUNK, CHUNK), 0)
        c_idx = lax.broadcasted_iota(jnp.int32, (CHUNK, CHUNK), 1)
        for g in range(GMLP_GROUPS):
            wm_ref[g] = jnp.where(c_idx <= t_idx, w_s_ref[g], 0.0).astype(BF16)
            bias_ref[g] = jnp.broadcast_to(b_st_ref[:, g:g + 1], (CHUNK, CHUNK))

    tiles = _row_tiles(h_ref.shape[0], SUB_GMLP)
    ns = [_rms(h_ref[r, :], pre_g_ref[...]).astype(BF16) for r in tiles]
    cols = 512
    for n, r in zip(ns, tiles):
        for c in range(GMLP_HALF // cols):
            z = jnp.dot(n, w_in_ref[:, GMLP_HALF + c * cols:GMLP_HALF + (c + 1) * cols],
                        preferred_element_type=F32)
            v_ref[r, c * cols:(c + 1) * cols] = jax.nn.gelu(z)
    for r in tiles:
        v = v_ref[r, :]
        mu = jnp.mean(v, axis=-1, keepdims=True)
        xc = v - mu
        var = jnp.mean(xc * xc, axis=-1, keepdims=True)
        vn = xc * lax.rsqrt(var + LN_EPS) * ln_g_ref[...] + ln_b_ref[...]
        vn_ref[r, :] = vn.astype(BF16)
    for n, r in zip(ns, tiles):
        for gp in range(GMLP_GROUPS // 2):
            u = jax.nn.gelu(jnp.dot(n, w_in_ref[:, gp * 2 * CHUNK:(gp + 1) * 2 * CHUNK],
                                    preferred_element_type=F32))
            for gg in range(2):
                g = 2 * gp + gg
                gl = slice(g * CHUNK, (g + 1) * CHUNK)
                for ci in range(SUB_GMLP // CHUNK):
                    rc = slice(r.start + ci * CHUNK, r.start + (ci + 1) * CHUNK)
                    sv = jnp.dot(wm_ref[g], vn_ref[rc, gl], preferred_element_type=F32) + bias_ref[g]
                    u_blk = u[ci * CHUNK:(ci + 1) * CHUNK, gg * CHUNK:(gg + 1) * CHUNK]
                    gated_ref[rc, gl] = (u_blk * sv).astype(BF16)
    for r in tiles:
        m = jnp.dot(gated_ref[r, :], w_out_ref[...], preferred_element_type=F32)
        o_ref[r, :] = h_ref[r, :] + _rms(m, post_g_ref[...])


def _gmlp(h, pre_g, post_g, w_in, ln_g, ln_b, w_s, b_s, w_out):
    t_total = h.shape[0]
    tm = TM_GMLP
    return pl.pallas_call(
        _gmlp_kernel,
        grid=(t_total // tm,),
        in_specs=[_row_spec(tm, D_MODEL), _const_spec((1, D_MODEL)), _const_spec((1, D_MODEL)),
                  _const_spec((D_MODEL, 2 * GMLP_HALF)),
                  _const_spec((1, GMLP_HALF)), _const_spec((1, GMLP_HALF)),
                  _const_spec((GMLP_GROUPS, CHUNK, CHUNK)), _const_spec((CHUNK, GMLP_GROUPS)),
                  _const_spec((GMLP_HALF, D_MODEL))],
        out_specs=_row_spec(tm, D_MODEL),
        out_shape=jax.ShapeDtypeStruct((t_total, D_MODEL), F32),
        scratch_shapes=[pltpu.VMEM((GMLP_GROUPS, CHUNK, CHUNK), BF16),
                        pltpu.VMEM((GMLP_GROUPS, CHUNK, CHUNK), F32),
                        pltpu.VMEM((tm, GMLP_HALF), F32), pltpu.VMEM((tm, GMLP_HALF), BF16),
                        pltpu.VMEM((tm, GMLP_HALF), BF16)],
        compiler_params=pltpu.CompilerParams(dimension_semantics=("arbitrary",),
                                             vmem_limit_bytes=VMEM_LIMIT),
        name="gmlp",
    )(h, pre_g.reshape(1, D_MODEL), post_g.reshape(1, D_MODEL), w_in,
      ln_g.reshape(1, GMLP_HALF), ln_b.reshape(1, GMLP_HALF), w_s, b_s.T, w_out)


def _attn_kernel(q_ref, knt_ref, krt_ref, v_ref, o_ref, kt_ref, va_ref):
    seq = q_ref.shape[0]
    n_blk = seq // TQ
    for hd in range(HEADS_PER_STEP):
        kt_ref[hd, :QK_NOPE, :] = knt_ref[hd * QK_NOPE:(hd + 1) * QK_NOPE, :]
        kt_ref[hd, QK_NOPE:QK_NOPE + QK_ROPE, :] = krt_ref[...]
        kt_ref[hd, QK_NOPE + QK_ROPE:, :] = jnp.zeros((HEAD_PAD - QK_NOPE - QK_ROPE, seq), BF16)
        va_ref[hd, :, :V_DIM] = v_ref[:, hd * V_DIM:(hd + 1) * V_DIM]
        va_ref[hd, :, V_DIM:] = jnp.ones((seq, V_DIM), BF16)
    r_idx = lax.broadcasted_iota(jnp.int32, (TQ, TQ), 0)
    c_idx = lax.broadcasted_iota(jnp.int32, (TQ, TQ), 1)
    causal = c_idx <= r_idx

    def scores(hd, qi):
        return jnp.dot(q_ref[qi * TQ:(qi + 1) * TQ, hd * HEAD_PAD:(hd + 1) * HEAD_PAD],
                       kt_ref[hd, :, :(qi + 1) * TQ], preferred_element_type=F32)

    tasks = [(hd, qi) for qi in range(n_blk) for hd in range(HEADS_PER_STEP)]
    pending = [scores(*t) for t in tasks[:HEADS_PER_STEP]]
    for k, (hd, qi) in enumerate(tasks):
        q0, q1 = qi * TQ, (qi + 1) * TQ
        s = pending.pop(0)
        if k + HEADS_PER_STEP < len(tasks):
            pending.append(scores(*tasks[k + HEADS_PER_STEP]))
        s_dg = jnp.where(causal, s[:, q0:q1], NEG_INF)
        m = jnp.max(s_dg, axis=-1, keepdims=True)
        if qi > 0:
            m = jnp.maximum(m, jnp.max(s[:, :q0], axis=-1, keepdims=True))
        p = jnp.exp2(s_dg - m).astype(BF16)
        if qi > 0:
            p = jnp.concatenate([jnp.exp2(s[:, :q0] - m).astype(BF16), p], axis=1)
        acc = jnp.dot(p, va_ref[hd, :q1, :], preferred_element_type=F32)
        o_ref[q0:q1, hd * V_DIM:(hd + 1) * V_DIM] = (acc[:, :V_DIM] / acc[:, V_DIM:]).astype(BF16)


def _attention(q, knt, krt, v, batch, seq):
    q3 = q.reshape(batch, seq, N_HEADS * HEAD_PAD)
    v3 = v.reshape(batch, seq, N_HEADS * V_DIM)
    hps = HEADS_PER_STEP
    out = pl.pallas_call(
        _attn_kernel,
        grid=(batch, N_HEADS // hps),
        in_specs=[pl.BlockSpec((None, seq, hps * HEAD_PAD), lambda b, h: (b, 0, h)),
                  pl.BlockSpec((None, hps * QK_NOPE, seq), lambda b, h: (b, h, 0)),
                  pl.BlockSpec((None, QK_ROPE, seq), lambda b, h: (b, 0, 0)),
                  pl.BlockSpec((None, seq, hps * V_DIM), lambda b, h: (b, 0, h))],
        out_specs=pl.BlockSpec((None, seq, hps * V_DIM), lambda b, h: (b, 0, h)),
        out_shape=jax.ShapeDtypeStruct((batch, seq, N_HEADS * V_DIM), BF16),
        scratch_shapes=[pltpu.VMEM((hps, HEAD_PAD, seq), BF16),
                        pltpu.VMEM((hps, seq, 2 * V_DIM), BF16)],
        compiler_params=_params(2),
        name="mla_attention",
    )(q3, knt, krt, v3)
    return out.reshape(batch * seq, N_HEADS * V_DIM)


def _swap_halves(w):
    half = w.shape[-1] // 2
    return jnp.concatenate([w[..., half:], w[..., :half]], axis=-1)


def _kv_weights(w_dkv, w_ukv):
    w_c = w_dkv[:, :KV_RANK].astype(BF16)
    w_r = w_dkv[:, KV_RANK:]
    w_rope = jnp.concatenate([w_r, _swap_halves(w_r)], axis=1).astype(BF16)
    w4 = w_ukv.reshape(KV_RANK, N_HEADS, QK_NOPE + V_DIM)
    w_uk_t = w4[:, :, :QK_NOPE].reshape(KV_RANK, N_HEADS * QK_NOPE).T.astype(BF16)
    w_uv = w4[:, :, QK_NOPE:].reshape(KV_RANK, N_HEADS * V_DIM).astype(BF16)
    return w_c, w_rope, w_uk_t, w_uv


def _q_weights(w_uq):
    w4 = w_uq.reshape(Q_RANK, N_HEADS, QK_NOPE + QK_ROPE)
    w_r = w4[:, :, QK_NOPE:]
    w = jnp.concatenate([w4[:, :, :QK_NOPE], w_r, _swap_halves(w_r)], axis=-1)
    return w.reshape(Q_RANK, N_HEADS * HEAD_PAD).astype(BF16)


def kernel(x, positions, ffn_pre_g, ffn_post_g, ffn_w_gate, ffn_w_up, ffn_w_down, mix_pre_g, mix_post_g, gmlp_w_in, gmlp_ln_g, gmlp_ln_b, gmlp_w_s, gmlp_b_s, gmlp_w_out, kv_norm_g, w_dkv, kv_a_norm_g, w_ukv, mla_w_dq, mla_q_norm_g, mla_w_uq, mla_w_o):
    batch, seq, _ = x.shape
    h = x.reshape(batch * seq, D_MODEL)
    tab = _rope_table(positions)
    depth = ffn_pre_g.shape[0]
    ffn_w = (ffn_pre_g.reshape(depth, 2, 1, D_MODEL), ffn_post_g.reshape(depth, 2, 1, D_MODEL),
             ffn_w_gate.astype(BF16), ffn_w_up.astype(BF16), ffn_w_down.astype(BF16))

    h = _ffn(h, ffn_w, 0, 0)
    h = _gmlp(h, mix_pre_g[0], mix_post_g[0], gmlp_w_in[0].astype(BF16), gmlp_ln_g[0],
              gmlp_ln_b[0], gmlp_w_s[0], gmlp_b_s[0], gmlp_w_out[0].astype(BF16))
    w_c, w_rope, w_uk_t, w_uv = _kv_weights(w_dkv, w_ukv)
    h, knt, krt, v = _ffn_kv(h, ffn_w, 0, 1, batch, seq, kv_norm_g, w_c, w_rope, kv_a_norm_g,
                             w_uk_t, w_uv, tab)
    h, q = _ffn_q(h, ffn_w, 1, 0, mix_pre_g[1], mla_w_dq[0].astype(BF16), mla_q_norm_g[0],
                  _q_weights(mla_w_uq[0]), tab)
    att = _attention(q, knt, krt, v, batch, seq)
    h = _oproj_ffn(h, att, mla_w_o[0].astype(BF16), mix_post_g[1], ffn_w, 1, 1)
    return h.reshape(batch, seq, D_MODEL)
```

```python
import jax
import jax.numpy as jnp
from jax import lax
from jax.experimental import pallas as pl
from jax.experimental.pallas import tpu as pltpu

D_MODEL = 1024
D_FF = 2816
CHUNK = 128
GMLP_HALF = 2 * D_MODEL
GMLP_GROUPS = 16
N_HEADS = 8
QK_NOPE = 128
QK_ROPE = 64
V_DIM = 128
KV_RANK = 256
Q_RANK = 512
ROPE_THETA = 10000.0
RMS_EPS = 1e-6
LN_EPS = 1e-5
NEG_INF = -1e30
QK_SCALE_LOG2E = (QK_NOPE + QK_ROPE) ** -0.5 * 1.4426950408889634

HEAD_PAD = 256
FF_COLS = 256
W_COLS = 256
TM_FFN = 1024
SUB_FFN = 256
TM_GMLP = 1024
SUB_GMLP = 256
TM_FUSED = 1024
SUB_FUSED = 256
TQ = 256
HEADS_PER_STEP = 2
VMEM_LIMIT = 56 * 1024 * 1024

BF16 = jnp.bfloat16
F32 = jnp.float32


def _rms(x, g):
    return x * lax.rsqrt(jnp.mean(x * x, axis=-1, keepdims=True) + RMS_EPS) * g


def _const_spec(shape):
    return pl.BlockSpec(shape, lambda *_: (0,) * len(shape), pipeline_mode=pl.Buffered(1))


def _row_spec(tm, width):
    return pl.BlockSpec((tm, width), lambda i: (i, 0))


def _row_tiles(rows, sub):
    return [slice(i * sub, (i + 1) * sub) for i in range(rows // sub)]


def _col_tiles(w):
    *lead, k, n = w.shape
    w = w.reshape(*lead, k, n // W_COLS, W_COLS)
    return jnp.swapaxes(w, -3, -2).astype(BF16)


def _dot_col_tiles(x, w_ref):
    return jnp.concatenate([jnp.dot(x, w_ref[c], preferred_element_type=F32)
                            for c in range(w_ref.shape[0])], axis=1)


def _params(n_axes):
    return pltpu.CompilerParams(dimension_semantics=("parallel",) * n_axes,
                                vmem_limit_bytes=VMEM_LIMIT)


def _rope_tab_kernel(pos_ref, invf_ref, tab_ref):
    ang = invf_ref[...] * pos_ref[...]
    c = jnp.cos(ang)
    s = jnp.sin(ang)
    tab_ref[...] = jnp.concatenate([c, c, -s, s], axis=0).T


def _rope_table(positions):
    t_total = positions.size
    tm = 2048
    pos = positions.reshape(1, t_total).astype(F32)
    inv_freq = ROPE_THETA ** (-jnp.arange(0, QK_ROPE, 2, dtype=F32) / QK_ROPE)
    return pl.pallas_call(
        _rope_tab_kernel,
        grid=(t_total // tm,),
        in_specs=[pl.BlockSpec((1, tm), lambda i: (0, i)),
                  _const_spec((QK_ROPE // 2, 1))],
        out_specs=pl.BlockSpec((tm, 2 * QK_ROPE), lambda i: (i, 0)),
        out_shape=jax.ShapeDtypeStruct((t_total, 2 * QK_ROPE), F32),
        compiler_params=_params(1),
        name="rope_table",
    )(pos, inv_freq.reshape(QK_ROPE // 2, 1))


def _rope(x2, tab):
    prod = x2 * tab
    return prod + pltpu.roll(prod, QK_ROPE, axis=1)


def _ffn_up(x, pre_g_ref, wg_ref, wu_ref, a_ref, r):
    n = _rms(x, pre_g_ref[...]).astype(BF16)
    for c in range(D_FF // FF_COLS):
        sl = slice(c * FF_COLS, (c + 1) * FF_COLS)
        g = jnp.dot(n, wg_ref[:, sl], preferred_element_type=F32)
        u = jnp.dot(n, wu_ref[:, sl], preferred_element_type=F32)
        a_ref[r, sl] = (g * jax.nn.sigmoid(g) * u).astype(BF16)


def _ffn_down(x, post_g_ref, wd_ref, a_ref, r):
    f = _dot_col_tiles(a_ref[r, :], wd_ref)
    return x + 0.5 * _rms(f, post_g_ref[...])


def _kv_stage(xs, tiles, g_ref, w_c_ref, w_rope_ref, ga_ref, w_uk_t_ref, w_uv_ref, tab_ref,
              knt_ref, krt_ref, v_ref):
    nt = (((1,), (1,)), ((), ()))
    ns = [_rms(x, g_ref[...]).astype(BF16) for x in xs]
    cs = [_rms(jnp.dot(n, w_c_ref[...], preferred_element_type=F32), ga_ref[...]).astype(BF16)
          for n in ns]
    for n, c, r in zip(ns, cs, tiles):
        k_t = lax.dot_general(w_uk_t_ref[...], c, nt, preferred_element_type=F32)
        knt_ref[:, r] = k_t.astype(BF16)
        v_ref[r, :] = jnp.dot(c, w_uv_ref[...], preferred_element_type=F32).astype(BF16)
        k2 = jnp.dot(n, w_rope_ref[...], preferred_element_type=F32)
        krt_ref[:, r] = _rope(k2, tab_ref[r, :]).T[:QK_ROPE].astype(BF16)


def _q_stage(xs, tiles, g_ref, w_dq_ref, gq_ref, w_uq_ref, tab_ref, q_ref):
    ns = [_rms(x, g_ref[...]).astype(BF16) for x in xs]
    qns = [_rms(jnp.dot(n, w_dq_ref[...], preferred_element_type=F32), gq_ref[...]).astype(BF16)
           for n in ns]
    for qn, r in zip(qns, tiles):
        tab = tab_ref[r, :]
        for hh in range(N_HEADS):
            base = hh * HEAD_PAD
            q = jnp.dot(qn, w_uq_ref[hh], preferred_element_type=F32)
            q = q * QK_SCALE_LOG2E
            q_ref[r, base:base + QK_NOPE] = q[:, :QK_NOPE].astype(BF16)
            q_ref[r, base + QK_NOPE:base + HEAD_PAD] = _rope(q[:, QK_NOPE:], tab).astype(BF16)


def _ffn_kernel(h_ref, pre_g_ref, post_g_ref, wg_ref, wu_ref, wd_ref, o_ref, a_ref):
    tiles = _row_tiles(h_ref.shape[0], SUB_FFN)
    for r in tiles:
        _ffn_up(h_ref[r, :], pre_g_ref, wg_ref, wu_ref, a_ref, r)
    for r in tiles:
        o_ref[r, :] = _ffn_down(h_ref[r, :], post_g_ref, wd_ref, a_ref, r)


def _ffn_kv_kernel(h_ref, pre_g_ref, post_g_ref, wg_ref, wu_ref, wd_ref,
                   g_ref, w_c_ref, w_rope_ref, ga_ref, w_uk_t_ref, w_uv_ref, tab_ref,
                   o_ref, knt_ref, krt_ref, v_ref, a_ref):
    tiles = _row_tiles(h_ref.shape[0], SUB_FUSED)
    for r in tiles:
        _ffn_up(h_ref[r, :], pre_g_ref, wg_ref, wu_ref, a_ref, r)
    for r in tiles:
        o_ref[r, :] = _ffn_down(h_ref[r, :], post_g_ref, wd_ref, a_ref, r)
    _kv_stage([o_ref[r, :] for r in tiles], tiles, g_ref, w_c_ref, w_rope_ref, ga_ref,
              w_uk_t_ref, w_uv_ref, tab_ref, knt_ref, krt_ref, v_ref)


def _ffn_q_kernel(h_ref, pre_g_ref, post_g_ref, wg_ref, wu_ref, wd_ref,
                  g_ref, w_dq_ref, gq_ref, w_uq_ref, tab_ref, o_ref, q_ref, a_ref):
    tiles = _row_tiles(h_ref.shape[0], SUB_FUSED)
    for r in tiles:
        _ffn_up(h_ref[r, :], pre_g_ref, wg_ref, wu_ref, a_ref, r)
    for r in tiles:
        o_ref[r, :] = _ffn_down(h_ref[r, :], post_g_ref, wd_ref, a_ref, r)
    _q_stage([o_ref[r, :] for r in tiles], tiles, g_ref, w_dq_ref, gq_ref, w_uq_ref, tab_ref, q_ref)


def _oproj_ffn_kernel(h_ref, att_ref, w_o_ref, g_mix_ref, pre_g_ref, post_g_ref, wg_ref, wu_ref,
                      wd_ref, o_ref, a_ref, h1_ref):
    tiles = _row_tiles(h_ref.shape[0], SUB_FUSED)
    for r in tiles:
        m = _dot_col_tiles(att_ref[r, :], w_o_ref)
        h1_ref[r, :] = h_ref[r, :] + _rms(m, g_mix_ref[...])
    for r in tiles:
        _ffn_up(h1_ref[r, :], pre_g_ref, wg_ref, wu_ref, a_ref, r)
    for r in tiles:
        o_ref[r, :] = _ffn_down(h1_ref[r, :], post_g_ref, wd_ref, a_ref, r)


def _ffn_weight_specs(layer, j):
    def stacked(*shape):
        return pl.BlockSpec((None, None) + shape, lambda *_: (layer, j) + (0,) * len(shape),
                            pipeline_mode=pl.Buffered(1))
    return [stacked(1, D_MODEL), stacked(1, D_MODEL), stacked(D_MODEL, D_FF),
            stacked(D_MODEL, D_FF), stacked(D_MODEL // W_COLS, D_FF, W_COLS)]


def _ffn(h, ffn_w, layer, j):
    t_total = h.shape[0]
    tm = TM_FFN
    return pl.pallas_call(
        _ffn_kernel,
        grid=(t_total // tm,),
        in_specs=[_row_spec(tm, D_MODEL)] + _ffn_weight_specs(layer, j),
        out_specs=_row_spec(tm, D_MODEL),
        out_shape=jax.ShapeDtypeStruct((t_total, D_MODEL), F32),
        scratch_shapes=[pltpu.VMEM((tm, D_FF), BF16)],
        compiler_params=_params(1),
        name="ffn",
    )(h, *ffn_w)


def _ffn_kv(h, ffn_w, layer, j, batch, seq, g, w_c, w_rope, ga, w_uk_t, w_uv, tab):
    t_total = h.shape[0]
    tm = TM_FUSED
    per_b = seq // tm
    return pl.pallas_call(
        _ffn_kv_kernel,
        grid=(t_total // tm,),
        in_specs=[_row_spec(tm, D_MODEL)] + _ffn_weight_specs(layer, j) + [
            _const_spec((1, D_MODEL)), _const_spec((D_MODEL, KV_RANK)),
            _const_spec((D_MODEL, 2 * QK_ROPE)), _const_spec((1, KV_RANK)),
            _const_spec((N_HEADS * QK_NOPE, KV_RANK)), _const_spec((KV_RANK, N_HEADS * V_DIM)),
            _row_spec(tm, 2 * QK_ROPE)],
        out_specs=[_row_spec(tm, D_MODEL),
                   pl.BlockSpec((None, N_HEADS * QK_NOPE, tm), lambda i: (i // per_b, 0, i % per_b)),
                   pl.BlockSpec((None, QK_ROPE, tm), lambda i: (i // per_b, 0, i % per_b)),
                   _row_spec(tm, N_HEADS * V_DIM)],
        out_shape=[jax.ShapeDtypeStruct((t_total, D_MODEL), F32),
                   jax.ShapeDtypeStruct((batch, N_HEADS * QK_NOPE, seq), BF16),
                   jax.ShapeDtypeStruct((batch, QK_ROPE, seq), BF16),
                   jax.ShapeDtypeStruct((t_total, N_HEADS * V_DIM), BF16)],
        scratch_shapes=[pltpu.VMEM((tm, D_FF), BF16)],
        compiler_params=_params(1),
        name="ffn_kv",
    )(h, *ffn_w, g.reshape(1, D_MODEL), w_c, w_rope, ga.reshape(1, KV_RANK), w_uk_t, w_uv, tab)


def _ffn_q(h, ffn_w, layer, j, g, w_dq, gq, w_uq, tab):
    t_total = h.shape[0]
    tm = TM_FUSED
    return pl.pallas_call(
        _ffn_q_kernel,
        grid=(t_total // tm,),
        in_specs=[_row_spec(tm, D_MODEL)] + _ffn_weight_specs(layer, j) + [
            _const_spec((1, D_MODEL)), _const_spec((D_MODEL, Q_RANK)), _const_spec((1, Q_RANK)),
            _const_spec((N_HEADS, Q_RANK, HEAD_PAD)), _row_spec(tm, 2 * QK_ROPE)],
        out_specs=[_row_spec(tm, D_MODEL), _row_spec(tm, N_HEADS * HEAD_PAD)],
        out_shape=[jax.ShapeDtypeStruct((t_total, D_MODEL), F32),
                   jax.ShapeDtypeStruct((t_total, N_HEADS * HEAD_PAD), BF16)],
        scratch_shapes=[pltpu.VMEM((tm, D_FF), BF16)],
        compiler_params=_params(1),
        name="ffn_q",
    )(h, *ffn_w, g.reshape(1, D_MODEL), w_dq, gq.reshape(1, Q_RANK), w_uq, tab)


def _oproj_ffn(h, att, w_o, g_mix, ffn_w, layer, j):
    t_total = h.shape[0]
    tm = TM_FUSED
    return pl.pallas_call(
        _oproj_ffn_kernel,
        grid=(t_total // tm,),
        in_specs=[_row_spec(tm, D_MODEL), _row_spec(tm, N_HEADS * V_DIM),
                  _const_spec((D_MODEL // W_COLS, N_HEADS * V_DIM, W_COLS)),
                  _const_spec((1, D_MODEL))]
                 + _ffn_weight_specs(layer, j),
        out_specs=_row_spec(tm, D_MODEL),
        out_shape=jax.ShapeDtypeStruct((t_total, D_MODEL), F32),
        scratch_shapes=[pltpu.VMEM((tm, D_FF), BF16), pltpu.VMEM((tm, D_MODEL), F32)],
        compiler_params=_params(1),
        name="oproj_ffn",
    )(h, att, w_o, g_mix.reshape(1, D_MODEL), *ffn_w)


_GELU_A = -2.0 * 0.7978845608028654 * 1.4426950408889634
_GELU_B = _GELU_A * 0.044715


def _gelu(x):
    return x / (1.0 + jnp.exp2(x * (_GELU_A + _GELU_B * (x * x))))


def _gmlp_kernel(h_ref, pre_g_ref, post_g_ref, w_in_ref, ln_g_ref, ln_b_ref, w_s_ref, b_st_ref,
                 w_out_ref, o_ref, wm_ref, bias_ref, n_ref, v_ref, vn_ref, gated_ref):
    @pl.when(pl.program_id(0) == 0)
    def _():
        t_idx = lax.broadcasted_iota(jnp.int32, (CHUNK, CHUNK), 0)
        c_idx = lax.broadcasted_iota(jnp.int32, (CHUNK, CHUNK), 1)
        for g in range(GMLP_GROUPS):
            wm_ref[g] = jnp.where(c_idx <= t_idx, w_s_ref[g], 0.0).astype(BF16)
            bias_ref[g] = jnp.broadcast_to(b_st_ref[:, g:g + 1], (CHUNK, CHUNK))

    tiles = _row_tiles(h_ref.shape[0], SUB_GMLP)
    for r in tiles:
        n_ref[r, :] = _rms(h_ref[r, :], pre_g_ref[...]).astype(BF16)
    n_half = GMLP_HALF // W_COLS
    for r in tiles:
        for c in range(n_half):
            z = jnp.dot(n_ref[r, :], w_in_ref[n_half + c], preferred_element_type=F32)
            v_ref[r, c * W_COLS:(c + 1) * W_COLS] = _gelu(z)
    for r in tiles:
        v = v_ref[r, :]
        mu = jnp.mean(v, axis=-1, keepdims=True)
        xc = v - mu
        var = jnp.mean(xc * xc, axis=-1, keepdims=True)
        vn = xc * lax.rsqrt(var + LN_EPS) * ln_g_ref[...] + ln_b_ref[...]
        vn_ref[r, :] = vn.astype(BF16)
    for r in tiles:
        for gp in range(n_half):
            u = _gelu(jnp.dot(n_ref[r, :], w_in_ref[gp], preferred_element_type=F32))
            for gg in range(2):
                g = 2 * gp + gg
                gl = slice(g * CHUNK, (g + 1) * CHUNK)
                for ci in range(SUB_GMLP // CHUNK):
                    rc = slice(r.start + ci * CHUNK, r.start + (ci + 1) * CHUNK)
                    sv = jnp.dot(wm_ref[g], vn_ref[rc, gl], preferred_element_type=F32) + bias_ref[g]
                    u_blk = u[ci * CHUNK:(ci + 1) * CHUNK, gg * CHUNK:(gg + 1) * CHUNK]
                    gated_ref[rc, gl] = (u_blk * sv).astype(BF16)
    for r in tiles:
        m = _dot_col_tiles(gated_ref[r, :], w_out_ref)
        o_ref[r, :] = h_ref[r, :] + _rms(m, post_g_ref[...])


def _gmlp(h, pre_g, post_g, w_in, ln_g, ln_b, w_s, b_s, w_out):
    t_total = h.shape[0]
    tm = TM_GMLP
    return pl.pallas_call(
        _gmlp_kernel,
        grid=(t_total // tm,),
        in_specs=[_row_spec(tm, D_MODEL), _const_spec((1, D_MODEL)), _const_spec((1, D_MODEL)),
                  _const_spec((2 * GMLP_HALF // W_COLS, D_MODEL, W_COLS)),
                  _const_spec((1, GMLP_HALF)), _const_spec((1, GMLP_HALF)),
                  _const_spec((GMLP_GROUPS, CHUNK, CHUNK)), _const_spec((CHUNK, GMLP_GROUPS)),
                  _const_spec((D_MODEL // W_COLS, GMLP_HALF, W_COLS))],
        out_specs=_row_spec(tm, D_MODEL),
        out_shape=jax.ShapeDtypeStruct((t_total, D_MODEL), F32),
        scratch_shapes=[pltpu.VMEM((GMLP_GROUPS, CHUNK, CHUNK), BF16),
                        pltpu.VMEM((GMLP_GROUPS, CHUNK, CHUNK), F32),
                        pltpu.VMEM((tm, D_MODEL), BF16),
                        pltpu.VMEM((tm, GMLP_HALF), F32), pltpu.VMEM((tm, GMLP_HALF), BF16),
                        pltpu.VMEM((tm, GMLP_HALF), BF16)],
        compiler_params=pltpu.CompilerParams(dimension_semantics=("arbitrary",),
                                             vmem_limit_bytes=VMEM_LIMIT),
        name="gmlp",
    )(h, pre_g.reshape(1, D_MODEL), post_g.reshape(1, D_MODEL), _col_tiles(w_in),
      ln_g.reshape(1, GMLP_HALF), ln_b.reshape(1, GMLP_HALF), w_s, b_s.T, _col_tiles(w_out))


def _attn_kernel(q_ref, knt_ref, krt_ref, v_ref, o_ref, kt_ref, va_ref):
    seq = q_ref.shape[0]
    n_blk = seq // TQ
    for hd in range(HEADS_PER_STEP):
        kt_ref[hd, :QK_NOPE, :] = knt_ref[hd * QK_NOPE:(hd + 1) * QK_NOPE, :]
        kt_ref[hd, QK_NOPE:QK_NOPE + QK_ROPE, :] = krt_ref[...]
        kt_ref[hd, QK_NOPE + QK_ROPE:, :] = jnp.zeros((HEAD_PAD - QK_NOPE - QK_ROPE, seq), BF16)
        va_ref[hd, :, :V_DIM] = v_ref[:, hd * V_DIM:(hd + 1) * V_DIM]
        va_ref[hd, :, V_DIM:] = jnp.ones((seq, V_DIM), BF16)
    r_idx = lax.broadcasted_iota(jnp.int32, (TQ, TQ), 0)
    c_idx = lax.broadcasted_iota(jnp.int32, (TQ, TQ), 1)
    causal = c_idx <= r_idx

    def scores(hd, qi):
        return jnp.dot(q_ref[qi * TQ:(qi + 1) * TQ, hd * HEAD_PAD:(hd + 1) * HEAD_PAD],
                       kt_ref[hd, :, :(qi + 1) * TQ], preferred_element_type=F32)

    tasks = [(hd, qi) for qi in range(n_blk) for hd in range(HEADS_PER_STEP)]
    pending = [scores(*t) for t in tasks[:HEADS_PER_STEP]]
    for k, (hd, qi) in enumerate(tasks):
        q0, q1 = qi * TQ, (qi + 1) * TQ
        s = pending.pop(0)
        if k + HEADS_PER_STEP < len(tasks):
            pending.append(scores(*tasks[k + HEADS_PER_STEP]))
        s_dg = jnp.where(causal, s[:, q0:q1], NEG_INF)
        m = jnp.max(s_dg, axis=-1, keepdims=True)
        if qi > 0:
            m = jnp.maximum(m, jnp.max(s[:, :q0], axis=-1, keepdims=True))
        p = jnp.exp2(s_dg - m).astype(BF16)
        if qi > 0:
            p = jnp.concatenate([jnp.exp2(s[:, :q0] - m).astype(BF16), p], axis=1)
        acc = jnp.dot(p, va_ref[hd, :q1, :], preferred_element_type=F32)
        o_ref[q0:q1, hd * V_DIM:(hd + 1) * V_DIM] = (acc[:, :V_DIM] / acc[:, V_DIM:]).astype(BF16)


def _attention(q, knt, krt, v, batch, seq):
    q3 = q.reshape(batch, seq, N_HEADS * HEAD_PAD)
    v3 = v.reshape(batch, seq, N_HEADS * V_DIM)
    hps = HEADS_PER_STEP
    out = pl.pallas_call(
        _attn_kernel,
        grid=(batch, N_HEADS // hps),
        in_specs=[pl.BlockSpec((None, seq, hps * HEAD_PAD), lambda b, h: (b, 0, h)),
                  pl.BlockSpec((None, hps * QK_NOPE, seq), lambda b, h: (b, h, 0)),
                  pl.BlockSpec((None, QK_ROPE, seq), lambda b, h: (b, 0, 0)),
                  pl.BlockSpec((None, seq, hps * V_DIM), lambda b, h: (b, 0, h))],
        out_specs=pl.BlockSpec((None, seq, hps * V_DIM), lambda b, h: (b, 0, h)),
        out_shape=jax.ShapeDtypeStruct((batch, seq, N_HEADS * V_DIM), BF16),
        scratch_shapes=[pltpu.VMEM((hps, HEAD_PAD, seq), BF16),
                        pltpu.VMEM((hps, seq, 2 * V_DIM), BF16)],
        compiler_params=_params(2),
        name="mla_attention",
    )(q3, knt, krt, v3)
    return out.reshape(batch * seq, N_HEADS * V_DIM)


def _swap_halves(w):
    half = w.shape[-1] // 2
    return jnp.concatenate([w[..., half:], w[..., :half]], axis=-1)


def _kv_weights(w_dkv, w_ukv):
    w_c = w_dkv[:, :KV_RANK].astype(BF16)
    w_r = w_dkv[:, KV_RANK:]
    w_rope = jnp.concatenate([w_r, _swap_halves(w_r)], axis=1).astype(BF16)
    w4 = w_ukv.reshape(KV_RANK, N_HEADS, QK_NOPE + V_DIM)
    w_uk_t = w4[:, :, :QK_NOPE].reshape(KV_RANK, N_HEADS * QK_NOPE).T.astype(BF16)
    w_uv = w4[:, :, QK_NOPE:].reshape(KV_RANK, N_HEADS * V_DIM).astype(BF16)
    return w_c, w_rope, w_uk_t, w_uv


def _q_weights(w_uq):
    w4 = w_uq.reshape(Q_RANK, N_HEADS, QK_NOPE + QK_ROPE)
    w_r = w4[:, :, QK_NOPE:]
    w = jnp.concatenate([w4[:, :, :QK_NOPE], w_r, _swap_halves(w_r)], axis=-1)
    return w.transpose(1, 0, 2).astype(BF16)


def kernel(x, positions, ffn_pre_g, ffn_post_g, ffn_w_gate, ffn_w_up, ffn_w_down, mix_pre_g, mix_post_g, gmlp_w_in, gmlp_ln_g, gmlp_ln_b, gmlp_w_s, gmlp_b_s, gmlp_w_out, kv_norm_g, w_dkv, kv_a_norm_g, w_ukv, mla_w_dq, mla_q_norm_g, mla_w_uq, mla_w_o):
    batch, seq, _ = x.shape
    h = x.reshape(batch * seq, D_MODEL)
    tab = _rope_table(positions)
    depth = ffn_pre_g.shape[0]
    ffn_w = (ffn_pre_g.reshape(depth, 2, 1, D_MODEL), ffn_post_g.reshape(depth, 2, 1, D_MODEL),
             ffn_w_gate.astype(BF16), ffn_w_up.astype(BF16), _col_tiles(ffn_w_down))

    h = _ffn(h, ffn_w, 0, 0)
    h = _gmlp(h, mix_pre_g[0], mix_post_g[0], gmlp_w_in[0], gmlp_ln_g[0],
              gmlp_ln_b[0], gmlp_w_s[0], gmlp_b_s[0], gmlp_w_out[0])
    w_c, w_rope, w_uk_t, w_uv = _kv_weights(w_dkv, w_ukv)
    h, knt, krt, v = _ffn_kv(h, ffn_w, 0, 1, batch, seq, kv_norm_g, w_c, w_rope, kv_a_norm_g,
                             w_uk_t, w_uv, tab)
    h, q = _ffn_q(h, ffn_w, 1, 0, mix_pre_g[1], mla_w_dq[0].astype(BF16), mla_q_norm_g[0],
                  _q_weights(mla_w_uq[0]), tab)
    att = _attention(q, knt, krt, v, batch, seq)
    h = _oproj_ffn(h, att, _col_tiles(mla_w_o[0]), mix_post_g[1], ffn_w, 1, 1)
    return h.reshape(batch, seq, D_MODEL)
```

```python
import functools
from typing import NamedTuple

import jax
import jax.numpy as jnp
from jax import lax
from jax.experimental import pallas as pl
from jax.experimental.pallas import tpu as pltpu

D_MODEL = 1024
D_FF = 2816
CHUNK = 128
GMLP_HALF = 2 * D_MODEL
GMLP_GROUPS = 16
N_HEADS = 8
QK_NOPE = 128
QK_ROPE = 64
V_DIM = 128
KV_RANK = 256
Q_RANK = 512
ROPE_THETA = 10000.0
RMS_EPS = 1e-6
LN_EPS = 1e-5
NEG_INF = -1e30
QK_SCALE_LOG2E = (QK_NOPE + QK_ROPE) ** -0.5 * 1.4426950408889634

HEAD_PAD = 256
FF_COLS = 256
W_COLS = 256
TM_FFN = 1024
SUB_FFN = 256
TM_GMLP = 1024
SUB_GMLP = 256
TM_FUSED = 1024
SUB_FUSED = 256
TM_ROPE = 2048
TQ = 256
HEADS_PER_STEP = 2
BF16_ROWS = 16
VMEM_LIMIT = 56 * 1024 * 1024

BF16 = jnp.bfloat16
F32 = jnp.float32


def _rms(x, g):
    return x * lax.rsqrt(jnp.mean(x * x, axis=-1, keepdims=True) + RMS_EPS) * g


def _const_spec(shape):
    return pl.BlockSpec(shape, lambda *_: (0,) * len(shape), pipeline_mode=pl.Buffered(1))


def _row_spec(tm, width):
    return pl.BlockSpec((tm, width), lambda i: (i, 0))


def _row_tiles(rows, sub):
    return [slice(i * sub, (i + 1) * sub) for i in range(rows // sub)]


def _col_tiles(w):
    k, n = w.shape
    return jnp.swapaxes(w.reshape(k, n // W_COLS, W_COLS), 0, 1).astype(BF16)


def _dot_col_tiles(x, w_ref):
    return jnp.concatenate([jnp.dot(x, w_ref[c], preferred_element_type=F32)
                            for c in range(w_ref.shape[0])], axis=1)


def _params(*semantics):
    return pltpu.CompilerParams(dimension_semantics=semantics, vmem_limit_bytes=VMEM_LIMIT)


class _Cast(NamedTuple):
    w: jax.Array
    lead: tuple
    col_tiled: bool


def _cast_plan(jobs, n_steps, step_of):
    in_specs, out_specs, out_shapes = [], [], []
    for job in jobs:
        k, n = job.w.shape[-2:]
        chunks = max(c for c in range(1, n_steps + 1) if k % (BF16_ROWS * c) == 0)
        rows = k // chunks

        def chunk(*idx, chunks=chunks):
            return jnp.minimum(step_of(*idx), chunks - 1)

        in_specs.append(pl.BlockSpec((None,) * len(job.lead) + (rows, n),
                                     lambda *idx, lead=job.lead, chunk=chunk: lead + (chunk(*idx), 0)))
        if job.col_tiled:
            out_specs.append(pl.BlockSpec((n // W_COLS, rows, W_COLS),
                                          lambda *idx, chunk=chunk: (0, chunk(*idx), 0)))
            out_shapes.append(jax.ShapeDtypeStruct((n // W_COLS, k, W_COLS), BF16))
        else:
            out_specs.append(pl.BlockSpec((rows, n), lambda *idx, chunk=chunk: (chunk(*idx), 0)))
            out_shapes.append(jax.ShapeDtypeStruct((k, n), BF16))
    return in_specs, out_specs, out_shapes


def _cast_blocks(src_refs, dst_refs):
    for src, dst in zip(src_refs, dst_refs):
        if len(dst.shape) == 3:
            for c in range(dst.shape[0]):
                dst[c] = src[:, c * W_COLS:(c + 1) * W_COLS].astype(BF16)
        else:
            dst[...] = src[...].astype(BF16)


def _split(refs, *counts):
    out, i = [], 0
    for c in counts:
        out.append(refs[i:i + c])
        i += c
    assert i == len(refs)
    return out


def _rope_tab_kernel(n_cast, *refs):
    (pos_ref, invf_ref), src, (tab_ref,), dst = _split(refs, 2, n_cast, 1, n_cast)
    _cast_blocks(src, dst)
    ang = invf_ref[...] * pos_ref[...]
    c = jnp.cos(ang)
    s = jnp.sin(ang)
    tab_ref[...] = jnp.concatenate([c, c, -s, s], axis=0).T


def _rope_table(positions, casts):
    t_total = positions.size
    tm = TM_ROPE
    pos = positions.reshape(1, t_total).astype(F32)
    inv_freq = ROPE_THETA ** (-jnp.arange(0, QK_ROPE, 2, dtype=F32) / QK_ROPE)
    c_in, c_out, c_shapes = _cast_plan(casts, t_total // tm, lambda i: i)
    return pl.pallas_call(
        functools.partial(_rope_tab_kernel, len(casts)),
        grid=(t_total // tm,),
        in_specs=[pl.BlockSpec((1, tm), lambda i: (0, i)), _const_spec((QK_ROPE // 2, 1))] + c_in,
        out_specs=[pl.BlockSpec((tm, 2 * QK_ROPE), lambda i: (i, 0))] + c_out,
        out_shape=[jax.ShapeDtypeStruct((t_total, 2 * QK_ROPE), F32)] + c_shapes,
        compiler_params=_params("arbitrary"),
        name="rope_table",
    )(pos, inv_freq.reshape(QK_ROPE // 2, 1), *[c.w for c in casts])


def _rope(x2, tab):
    prod = x2 * tab
    return prod + pltpu.roll(prod, QK_ROPE, axis=1)


def _ffn_up(x, pre_g_ref, wg_ref, wu_ref, a_ref, r):
    n = _rms(x, pre_g_ref[...]).astype(BF16)
    for c in range(D_FF // FF_COLS):
        sl = slice(c * FF_COLS, (c + 1) * FF_COLS)
        g = jnp.dot(n, wg_ref[:, sl], preferred_element_type=F32)
        u = jnp.dot(n, wu_ref[:, sl], preferred_element_type=F32)
        a_ref[r, sl] = (g * jax.nn.sigmoid(g) * u).astype(BF16)


def _ffn_down(x, post_g_ref, wd_ref, a_ref, r):
    f = _dot_col_tiles(a_ref[r, :], wd_ref)
    return x + 0.5 * _rms(f, post_g_ref[...])


def _kv_stage(xs, tiles, g_ref, w_c_ref, w_rope_ref, ga_ref, w_uk_t_ref, w_uv_ref, tab_ref,
              knt_ref, krt_ref, v_ref):
    nt = (((1,), (1,)), ((), ()))
    ns = [_rms(x, g_ref[...]).astype(BF16) for x in xs]
    cs = [_rms(jnp.dot(n, w_c_ref[...], preferred_element_type=F32), ga_ref[...]).astype(BF16)
          for n in ns]
    for n, c, r in zip(ns, cs, tiles):
        k_t = lax.dot_general(w_uk_t_ref[...], c, nt, preferred_element_type=F32)
        knt_ref[:, r] = k_t.astype(BF16)
        v_ref[r, :] = _dot_col_tiles(c, w_uv_ref).astype(BF16)
        k2 = jnp.dot(n, w_rope_ref[...], preferred_element_type=F32)
        krt_ref[:, r] = _rope(k2, tab_ref[r, :]).T[:QK_ROPE].astype(BF16)


def _q_stage(xs, tiles, g_ref, w_dq_ref, gq_ref, w_uq_ref, tab_ref, q_ref):
    ns = [_rms(x, g_ref[...]).astype(BF16) for x in xs]
    qns = [_rms(jnp.dot(n, w_dq_ref[...], preferred_element_type=F32), gq_ref[...]).astype(BF16)
           for n in ns]
    for qn, r in zip(qns, tiles):
        tab = tab_ref[r, :]
        for hh in range(N_HEADS):
            base = hh * HEAD_PAD
            q = jnp.dot(qn, w_uq_ref[hh], preferred_element_type=F32)
            q = q * QK_SCALE_LOG2E
            q_ref[r, base:base + QK_NOPE] = q[:, :QK_NOPE].astype(BF16)
            q_ref[r, base + QK_NOPE:base + HEAD_PAD] = _rope(q[:, QK_NOPE:], tab).astype(BF16)


def _ffn_kernel(n_cast, *refs):
    ((h_ref, pre_g_ref, post_g_ref, wg_ref, wu_ref, wd_ref), src, (o_ref,), dst,
     (a_ref,)) = _split(refs, 6, n_cast, 1, n_cast, 1)
    _cast_blocks(src, dst)
    tiles = _row_tiles(h_ref.shape[0], SUB_FFN)
    for r in tiles:
        _ffn_up(h_ref[r, :], pre_g_ref, wg_ref, wu_ref, a_ref, r)
    for r in tiles:
        o_ref[r, :] = _ffn_down(h_ref[r, :], post_g_ref, wd_ref, a_ref, r)


def _ffn_kv_kernel(h_ref, pre_g_ref, post_g_ref, wg_ref, wu_ref, wd_ref,
                   g_ref, w_c_ref, w_rope_ref, ga_ref, w_uk_t_ref, w_uv_ref, tab_ref,
                   o_ref, knt_ref, krt_ref, v_ref, a_ref):
    tiles = _row_tiles(h_ref.shape[0], SUB_FUSED)
    for r in tiles:
        _ffn_up(h_ref[r, :], pre_g_ref, wg_ref, wu_ref, a_ref, r)
    for r in tiles:
        o_ref[r, :] = _ffn_down(h_ref[r, :], post_g_ref, wd_ref, a_ref, r)
    _kv_stage([o_ref[r, :] for r in tiles], tiles, g_ref, w_c_ref, w_rope_ref, ga_ref,
              w_uk_t_ref, w_uv_ref, tab_ref, knt_ref, krt_ref, v_ref)


def _ffn_q_kernel(h_ref, pre_g_ref, post_g_ref, wg_ref, wu_ref, wd_ref,
                  g_ref, w_dq_ref, gq_ref, w_uq_ref, tab_ref, o_ref, q_ref, a_ref):
    tiles = _row_tiles(h_ref.shape[0], SUB_FUSED)
    for r in tiles:
        _ffn_up(h_ref[r, :], pre_g_ref, wg_ref, wu_ref, a_ref, r)
    for r in tiles:
        o_ref[r, :] = _ffn_down(h_ref[r, :], post_g_ref, wd_ref, a_ref, r)
    _q_stage([o_ref[r, :] for r in tiles], tiles, g_ref, w_dq_ref, gq_ref, w_uq_ref, tab_ref, q_ref)


def _oproj_ffn_kernel(h_ref, att_ref, w_o_ref, g_mix_ref, pre_g_ref, post_g_ref, wg_ref, wu_ref,
                      wd_ref, o_ref, a_ref, h1_ref):
    tiles = _row_tiles(h_ref.shape[0], SUB_FUSED)
    for r in tiles:
        m = _dot_col_tiles(att_ref[r, :], w_o_ref)
        h1_ref[r, :] = h_ref[r, :] + _rms(m, g_mix_ref[...])
    for r in tiles:
        _ffn_up(h1_ref[r, :], pre_g_ref, wg_ref, wu_ref, a_ref, r)
    for r in tiles:
        o_ref[r, :] = _ffn_down(h1_ref[r, :], post_g_ref, wd_ref, a_ref, r)


def _ffn_weight_specs(layer, j):
    gain = pl.BlockSpec((None, None, 1, D_MODEL), lambda *_: (layer, j, 0, 0),
                        pipeline_mode=pl.Buffered(1))
    return [gain, gain, _const_spec((D_MODEL, D_FF)), _const_spec((D_MODEL, D_FF)),
            _const_spec((D_MODEL // W_COLS, D_FF, W_COLS))]


def _ffn(h, gains, w, layer, j, casts):
    t_total = h.shape[0]
    tm = TM_FFN
    c_in, c_out, c_shapes = _cast_plan(casts, t_total // tm, lambda i: i)
    return pl.pallas_call(
        functools.partial(_ffn_kernel, len(casts)),
        grid=(t_total // tm,),
        in_specs=[_row_spec(tm, D_MODEL)] + _ffn_weight_specs(layer, j) + c_in,
        out_specs=[_row_spec(tm, D_MODEL)] + c_out,
        out_shape=[jax.ShapeDtypeStruct((t_total, D_MODEL), F32)] + c_shapes,
        scratch_shapes=[pltpu.VMEM((tm, D_FF), BF16)],
        compiler_params=_params("arbitrary"),
        name="ffn",
    )(h, *gains, *w, *[c.w for c in casts])


def _ffn_kv(h, gains, w, layer, j, batch, seq, g, w_c, w_rope, ga, w_uk_t, w_uv, tab):
    t_total = h.shape[0]
    tm = TM_FUSED
    per_b = seq // tm
    return pl.pallas_call(
        _ffn_kv_kernel,
        grid=(t_total // tm,),
        in_specs=[_row_spec(tm, D_MODEL)] + _ffn_weight_specs(layer, j) + [
            _const_spec((1, D_MODEL)), _const_spec((D_MODEL, KV_RANK)),
            _const_spec((D_MODEL, 2 * QK_ROPE)), _const_spec((1, KV_RANK)),
            _const_spec((N_HEADS * QK_NOPE, KV_RANK)),
            _const_spec((N_HEADS * V_DIM // W_COLS, KV_RANK, W_COLS)),
            _row_spec(tm, 2 * QK_ROPE)],
        out_specs=[_row_spec(tm, D_MODEL),
                   pl.BlockSpec((None, N_HEADS * QK_NOPE, tm), lambda i: (i // per_b, 0, i % per_b)),
                   pl.BlockSpec((None, QK_ROPE, tm), lambda i: (i // per_b, 0, i % per_b)),
                   _row_spec(tm, N_HEADS * V_DIM)],
        out_shape=[jax.ShapeDtypeStruct((t_total, D_MODEL), F32),
                   jax.ShapeDtypeStruct((batch, N_HEADS * QK_NOPE, seq), BF16),
                   jax.ShapeDtypeStruct((batch, QK_ROPE, seq), BF16),
                   jax.ShapeDtypeStruct((t_total, N_HEADS * V_DIM), BF16)],
        scratch_shapes=[pltpu.VMEM((tm, D_FF), BF16)],
        compiler_params=_params("parallel"),
        name="ffn_kv",
    )(h, *gains, *w, g.reshape(1, D_MODEL), w_c, w_rope, ga.reshape(1, KV_RANK), w_uk_t, w_uv, tab)


def _ffn_q(h, gains, w, layer, j, g, w_dq, gq, w_uq, tab):
    t_total = h.shape[0]
    tm = TM_FUSED
    return pl.pallas_call(
        _ffn_q_kernel,
        grid=(t_total // tm,),
        in_specs=[_row_spec(tm, D_MODEL)] + _ffn_weight_specs(layer, j) + [
            _const_spec((1, D_MODEL)), _const_spec((D_MODEL, Q_RANK)), _const_spec((1, Q_RANK)),
            _const_spec((N_HEADS, Q_RANK, HEAD_PAD)), _row_spec(tm, 2 * QK_ROPE)],
        out_specs=[_row_spec(tm, D_MODEL), _row_spec(tm, N_HEADS * HEAD_PAD)],
        out_shape=[jax.ShapeDtypeStruct((t_total, D_MODEL), F32),
                   jax.ShapeDtypeStruct((t_total, N_HEADS * HEAD_PAD), BF16)],
        scratch_shapes=[pltpu.VMEM((tm, D_FF), BF16)],
        compiler_params=_params("parallel"),
        name="ffn_q",
    )(h, *gains, *w, g.reshape(1, D_MODEL), w_dq, gq.reshape(1, Q_RANK), w_uq, tab)


def _oproj_ffn(h, att, w_o, g_mix, gains, w, layer, j):
    t_total = h.shape[0]
    tm = TM_FUSED
    return pl.pallas_call(
        _oproj_ffn_kernel,
        grid=(t_total // tm,),
        in_specs=[_row_spec(tm, D_MODEL), _row_spec(tm, N_HEADS * V_DIM),
                  _const_spec((D_MODEL // W_COLS, N_HEADS * V_DIM, W_COLS)),
                  _const_spec((1, D_MODEL))] + _ffn_weight_specs(layer, j),
        out_specs=_row_spec(tm, D_MODEL),
        out_shape=jax.ShapeDtypeStruct((t_total, D_MODEL), F32),
        scratch_shapes=[pltpu.VMEM((tm, D_FF), BF16), pltpu.VMEM((tm, D_MODEL), F32)],
        compiler_params=_params("parallel"),
        name="oproj_ffn",
    )(h, att, w_o, g_mix.reshape(1, D_MODEL), *gains, *w)


_GELU_A = -2.0 * 0.7978845608028654 * 1.4426950408889634
_GELU_B = _GELU_A * 0.044715


def _gelu(x):
    return x / (1.0 + jnp.exp2(x * (_GELU_A + _GELU_B * (x * x))))


def _gmlp_kernel(n_cast, *refs):
    ((h_ref, pre_g_ref, post_g_ref, w_in_ref, ln_g_ref, ln_b_ref, w_s_ref, b_st_ref, w_out_ref),
     src, (o_ref,), dst, (wm_ref, bias_ref, n_ref, v_ref, vn_ref, gated_ref)) = _split(
         refs, 9, n_cast, 1, n_cast, 6)
    _cast_blocks(src, dst)

    @pl.when(pl.program_id(0) == 0)
    def _():
        t_idx = lax.broadcasted_iota(jnp.int32, (CHUNK, CHUNK), 0)
        c_idx = lax.broadcasted_iota(jnp.int32, (CHUNK, CHUNK), 1)
        for g in range(GMLP_GROUPS):
            wm_ref[g] = jnp.where(c_idx <= t_idx, w_s_ref[g], 0.0).astype(BF16)
            bias_ref[g] = jnp.broadcast_to(b_st_ref[:, g:g + 1], (CHUNK, CHUNK))

    tiles = _row_tiles(h_ref.shape[0], SUB_GMLP)
    for r in tiles:
        n_ref[r, :] = _rms(h_ref[r, :], pre_g_ref[...]).astype(BF16)
    n_half = GMLP_HALF // W_COLS
    for r in tiles:
        for c in range(n_half):
            z = jnp.dot(n_ref[r, :], w_in_ref[n_half + c], preferred_element_type=F32)
            v_ref[r, c * W_COLS:(c + 1) * W_COLS] = _gelu(z)
    for r in tiles:
        v = v_ref[r, :]
        mu = jnp.mean(v, axis=-1, keepdims=True)
        xc = v - mu
        var = jnp.mean(xc * xc, axis=-1, keepdims=True)
        vn = xc * lax.rsqrt(var + LN_EPS) * ln_g_ref[...] + ln_b_ref[...]
        vn_ref[r, :] = vn.astype(BF16)
    for r in tiles:
        for gp in range(n_half):
            u = _gelu(jnp.dot(n_ref[r, :], w_in_ref[gp], preferred_element_type=F32))
            for gg in range(2):
                g = 2 * gp + gg
                gl = slice(g * CHUNK, (g + 1) * CHUNK)
                for ci in range(SUB_GMLP // CHUNK):
                    rc = slice(r.start + ci * CHUNK, r.start + (ci + 1) * CHUNK)
                    sv = jnp.dot(wm_ref[g], vn_ref[rc, gl], preferred_element_type=F32) + bias_ref[g]
                    u_blk = u[ci * CHUNK:(ci + 1) * CHUNK, gg * CHUNK:(gg + 1) * CHUNK]
                    gated_ref[rc, gl] = (u_blk * sv).astype(BF16)
    for r in tiles:
        m = _dot_col_tiles(gated_ref[r, :], w_out_ref)
        o_ref[r, :] = h_ref[r, :] + _rms(m, post_g_ref[...])


def _gmlp(h, pre_g, post_g, w_in, ln_g, ln_b, w_s, b_s, w_out, casts):
    t_total = h.shape[0]
    tm = TM_GMLP
    c_in, c_out, c_shapes = _cast_plan(casts, t_total // tm, lambda i: i)
    return pl.pallas_call(
        functools.partial(_gmlp_kernel, len(casts)),
        grid=(t_total // tm,),
        in_specs=[_row_spec(tm, D_MODEL), _const_spec((1, D_MODEL)), _const_spec((1, D_MODEL)),
                  _const_spec((2 * GMLP_HALF // W_COLS, D_MODEL, W_COLS)),
                  _const_spec((1, GMLP_HALF)), _const_spec((1, GMLP_HALF)),
                  _const_spec((GMLP_GROUPS, CHUNK, CHUNK)), _const_spec((CHUNK, GMLP_GROUPS)),
                  _const_spec((D_MODEL // W_COLS, GMLP_HALF, W_COLS))] + c_in,
        out_specs=[_row_spec(tm, D_MODEL)] + c_out,
        out_shape=[jax.ShapeDtypeStruct((t_total, D_MODEL), F32)] + c_shapes,
        scratch_shapes=[pltpu.VMEM((GMLP_GROUPS, CHUNK, CHUNK), BF16),
                        pltpu.VMEM((GMLP_GROUPS, CHUNK, CHUNK), F32),
                        pltpu.VMEM((tm, D_MODEL), BF16),
                        pltpu.VMEM((tm, GMLP_HALF), F32), pltpu.VMEM((tm, GMLP_HALF), BF16),
                        pltpu.VMEM((tm, GMLP_HALF), BF16)],
        compiler_params=_params("arbitrary"),
        name="gmlp",
    )(h, pre_g.reshape(1, D_MODEL), post_g.reshape(1, D_MODEL), w_in,
      ln_g.reshape(1, GMLP_HALF), ln_b.reshape(1, GMLP_HALF), w_s, b_s.T, w_out,
      *[c.w for c in casts])


def _attn_kernel(n_cast, *refs):
    ((q_ref, knt_ref, krt_ref, v_ref), src, (o_ref,), dst, (kt_ref, va_ref)) = _split(
        refs, 4, n_cast, 1, n_cast, 2)
    _cast_blocks(src, dst)
    seq = q_ref.shape[0]
    n_blk = seq // TQ
    for hd in range(HEADS_PER_STEP):
        kt_ref[hd, :QK_NOPE, :] = knt_ref[hd * QK_NOPE:(hd + 1) * QK_NOPE, :]
        kt_ref[hd, QK_NOPE:QK_NOPE + QK_ROPE, :] = krt_ref[...]
        kt_ref[hd, QK_NOPE + QK_ROPE:, :] = jnp.zeros((HEAD_PAD - QK_NOPE - QK_ROPE, seq), BF16)
        va_ref[hd, :, :V_DIM] = v_ref[:, hd * V_DIM:(hd + 1) * V_DIM]
        va_ref[hd, :, V_DIM:] = jnp.ones((seq, V_DIM), BF16)
    r_idx = lax.broadcasted_iota(jnp.int32, (TQ, TQ), 0)
    c_idx = lax.broadcasted_iota(jnp.int32, (TQ, TQ), 1)
    causal = c_idx <= r_idx

    def scores(hd, qi):
        return jnp.dot(q_ref[qi * TQ:(qi + 1) * TQ, hd * HEAD_PAD:(hd + 1) * HEAD_PAD],
                       kt_ref[hd, :, :(qi + 1) * TQ], preferred_element_type=F32)

    tasks = [(hd, qi) for qi in range(n_blk) for hd in range(HEADS_PER_STEP)]
    pending = [scores(*t) for t in tasks[:HEADS_PER_STEP]]
    for k, (hd, qi) in enumerate(tasks):
        q0, q1 = qi * TQ, (qi + 1) * TQ
        s = pending.pop(0)
        if k + HEADS_PER_STEP < len(tasks):
            pending.append(scores(*tasks[k + HEADS_PER_STEP]))
        s_dg = jnp.where(causal, s[:, q0:q1], NEG_INF)
        m = jnp.max(s_dg, axis=-1, keepdims=True)
        if qi > 0:
            m = jnp.maximum(m, jnp.max(s[:, :q0], axis=-1, keepdims=True))
        p = jnp.exp2(s_dg - m).astype(BF16)
        if qi > 0:
            p = jnp.concatenate([jnp.exp2(s[:, :q0] - m).astype(BF16), p], axis=1)
        acc = jnp.dot(p, va_ref[hd, :q1, :], preferred_element_type=F32)
        o_ref[q0:q1, hd * V_DIM:(hd + 1) * V_DIM] = (acc[:, :V_DIM] / acc[:, V_DIM:]).astype(BF16)


def _attention(q, knt, krt, v, batch, seq, casts):
    q3 = q.reshape(batch, seq, N_HEADS * HEAD_PAD)
    v3 = v.reshape(batch, seq, N_HEADS * V_DIM)
    hps = HEADS_PER_STEP
    groups = N_HEADS // hps
    c_in, c_out, c_shapes = _cast_plan(casts, batch * groups, lambda b, h: b * groups + h)
    out, *cast = pl.pallas_call(
        functools.partial(_attn_kernel, len(casts)),
        grid=(batch, groups),
        in_specs=[pl.BlockSpec((None, seq, hps * HEAD_PAD), lambda b, h: (b, 0, h)),
                  pl.BlockSpec((None, hps * QK_NOPE, seq), lambda b, h: (b, h, 0)),
                  pl.BlockSpec((None, QK_ROPE, seq), lambda b, h: (b, 0, 0)),
                  pl.BlockSpec((None, seq, hps * V_DIM), lambda b, h: (b, 0, h))] + c_in,
        out_specs=[pl.BlockSpec((None, seq, hps * V_DIM), lambda b, h: (b, 0, h))] + c_out,
        out_shape=[jax.ShapeDtypeStruct((batch, seq, N_HEADS * V_DIM), BF16)] + c_shapes,
        scratch_shapes=[pltpu.VMEM((hps, HEAD_PAD, seq), BF16),
                        pltpu.VMEM((hps, seq, 2 * V_DIM), BF16)],
        compiler_params=_params("arbitrary", "arbitrary"),
        name="mla_attention",
    )(q3, knt, krt, v3, *[c.w for c in casts])
    return out.reshape(batch * seq, N_HEADS * V_DIM), cast


def _swap_halves(w):
    half = w.shape[-1] // 2
    return jnp.concatenate([w[..., half:], w[..., :half]], axis=-1)


def _kv_weights(w_dkv, w_ukv):
    w_c = w_dkv[:, :KV_RANK].astype(BF16)
    w_r = w_dkv[:, KV_RANK:]
    w_rope = jnp.concatenate([w_r, _swap_halves(w_r)], axis=1).astype(BF16)
    w4 = w_ukv.reshape(KV_RANK, N_HEADS, QK_NOPE + V_DIM)
    w_uk_t = w4[:, :, :QK_NOPE].reshape(KV_RANK, N_HEADS * QK_NOPE).T.astype(BF16)
    w_uv = _col_tiles(w4[:, :, QK_NOPE:].reshape(KV_RANK, N_HEADS * V_DIM))
    return w_c, w_rope, w_uk_t, w_uv


def _q_weights(w_uq):
    w4 = w_uq.reshape(Q_RANK, N_HEADS, QK_NOPE + QK_ROPE)
    w_r = w4[:, :, QK_NOPE:]
    w = jnp.concatenate([w4[:, :, :QK_NOPE], w_r, _swap_halves(w_r)], axis=-1)
    return w.transpose(1, 0, 2).astype(BF16)


def kernel(x, positions, ffn_pre_g, ffn_post_g, ffn_w_gate, ffn_w_up, ffn_w_down, mix_pre_g, mix_post_g, gmlp_w_in, gmlp_ln_g, gmlp_ln_b, gmlp_w_s, gmlp_b_s, gmlp_w_out, kv_norm_g, w_dkv, kv_a_norm_g, w_ukv, mla_w_dq, mla_q_norm_g, mla_w_uq, mla_w_o):
    batch, seq, _ = x.shape
    h = x.reshape(batch * seq, D_MODEL)
    depth = ffn_pre_g.shape[0]
    gains = (ffn_pre_g.reshape(depth, 2, 1, D_MODEL), ffn_post_g.reshape(depth, 2, 1, D_MODEL))

    def ffn_casts(layer, j):
        return [_Cast(ffn_w_gate, (layer, j), False), _Cast(ffn_w_up, (layer, j), False),
                _Cast(ffn_w_down, (layer, j), True)]

    tab, *w00 = _rope_table(positions, ffn_casts(0, 0))
    h, w_in, w_out = _ffn(h, gains, w00, 0, 0,
                          [_Cast(gmlp_w_in, (0,), True), _Cast(gmlp_w_out, (0,), True)])
    h, *w_ffn = _gmlp(h, mix_pre_g[0], mix_post_g[0], w_in, gmlp_ln_g[0], gmlp_ln_b[0],
                      gmlp_w_s[0], gmlp_b_s[0], w_out, ffn_casts(0, 1) + ffn_casts(1, 0))
    w01, w10 = w_ffn[:3], w_ffn[3:]
    w_c, w_rope, w_uk_t, w_uv = _kv_weights(w_dkv, w_ukv)
    h, knt, krt, v = _ffn_kv(h, gains, w01, 0, 1, batch, seq, kv_norm_g, w_c, w_rope,
                             kv_a_norm_g, w_uk_t, w_uv, tab)
    h, q = _ffn_q(h, gains, w10, 1, 0, mix_pre_g[1], mla_w_dq[0].astype(BF16), mla_q_norm_g[0],
                  _q_weights(mla_w_uq[0]), tab)
    att, w11 = _attention(q, knt, krt, v, batch, seq, ffn_casts(1, 1))
    h = _oproj_ffn(h, att, _col_tiles(mla_w_o[0]), mix_post_g[1], gains, w11, 1, 1)
    return h.reshape(batch, seq, D_MODEL)
```

```python
import functools
from typing import NamedTuple

import jax
import jax.numpy as jnp
from jax import lax
from jax.experimental import pallas as pl
from jax.experimental.pallas import tpu as pltpu

D_MODEL = 1024
D_FF = 2816
CHUNK = 128
GMLP_HALF = 2 * D_MODEL
GMLP_GROUPS = 16
N_HEADS = 8
QK_NOPE = 128
QK_ROPE = 64
V_DIM = 128
KV_RANK = 256
Q_RANK = 512
ROPE_THETA = 10000.0
RMS_EPS = 1e-6
LN_EPS = 1e-5
NEG_INF = -1e30
QK_SCALE_LOG2E = (QK_NOPE + QK_ROPE) ** -0.5 * 1.4426950408889634

HEAD_PAD = 256
FF_COLS = 256
W_COLS = 256
TM_FFN = 1024
SUB_FFN = 256
TM_GMLP = 1024
SUB_GMLP = 256
TM_FUSED = 1024
SUB_FUSED = 256
TM_ROPE = 2048
TQ = 256
HEADS_PER_STEP = 4
BF16_ROWS = 16
VT_ROWS = V_DIM + BF16_ROWS
VMEM_LIMIT = 56 * 1024 * 1024

BF16 = jnp.bfloat16
F32 = jnp.float32


def _rms(x, g):
    return x * lax.rsqrt(jnp.mean(x * x, axis=-1, keepdims=True) + RMS_EPS) * g


def _const_spec(shape):
    return pl.BlockSpec(shape, lambda *_: (0,) * len(shape), pipeline_mode=pl.Buffered(1))


def _row_spec(tm, width):
    return pl.BlockSpec((tm, width), lambda i: (i, 0))


def _row_tiles(rows, sub):
    return [slice(i * sub, (i + 1) * sub) for i in range(rows // sub)]


def _col_tiles(w):
    k, n = w.shape
    return jnp.swapaxes(w.reshape(k, n // W_COLS, W_COLS), 0, 1).astype(BF16)


def _dot_col_tiles(x, w_ref):
    return jnp.concatenate([jnp.dot(x, w_ref[c], preferred_element_type=F32)
                            for c in range(w_ref.shape[0])], axis=1)


def _params(*semantics):
    return pltpu.CompilerParams(dimension_semantics=semantics, vmem_limit_bytes=VMEM_LIMIT)


class _Cast(NamedTuple):
    w: jax.Array
    lead: tuple
    col_tiled: bool


def _cast_plan(jobs, n_steps, step_of):
    in_specs, out_specs, out_shapes = [], [], []
    for job in jobs:
        k, n = job.w.shape[-2:]
        chunks = max(c for c in range(1, n_steps + 1) if k % (BF16_ROWS * c) == 0)
        rows = k // chunks

        def chunk(*idx, chunks=chunks):
            return jnp.minimum(step_of(*idx), chunks - 1)

        in_specs.append(pl.BlockSpec((None,) * len(job.lead) + (rows, n),
                                     lambda *idx, lead=job.lead, chunk=chunk: lead + (chunk(*idx), 0)))
        if job.col_tiled:
            out_specs.append(pl.BlockSpec((n // W_COLS, rows, W_COLS),
                                          lambda *idx, chunk=chunk: (0, chunk(*idx), 0)))
            out_shapes.append(jax.ShapeDtypeStruct((n // W_COLS, k, W_COLS), BF16))
        else:
            out_specs.append(pl.BlockSpec((rows, n), lambda *idx, chunk=chunk: (chunk(*idx), 0)))
            out_shapes.append(jax.ShapeDtypeStruct((k, n), BF16))
    return in_specs, out_specs, out_shapes


def _cast_blocks(src_refs, dst_refs):
    for src, dst in zip(src_refs, dst_refs):
        if len(dst.shape) == 3:
            for c in range(dst.shape[0]):
                dst[c] = src[:, c * W_COLS:(c + 1) * W_COLS].astype(BF16)
        else:
            dst[...] = src[...].astype(BF16)


def _split(refs, *counts):
    out, i = [], 0
    for c in counts:
        out.append(refs[i:i + c])
        i += c
    assert i == len(refs)
    return out


def _rope_tab_kernel(n_cast, *refs):
    (pos_ref, invf_ref), src, (tab_ref, tab_t_ref), dst = _split(refs, 2, n_cast, 2, n_cast)
    _cast_blocks(src, dst)
    ang = invf_ref[...] * pos_ref[...]
    c = jnp.cos(ang)
    s = jnp.sin(ang)
    t = jnp.concatenate([c, c, -s, s], axis=0)
    tab_t_ref[...] = t
    tab_ref[...] = t.T


def _rope_table(positions, casts):
    t_total = positions.size
    tm = TM_ROPE
    pos = positions.reshape(1, t_total).astype(F32)
    inv_freq = ROPE_THETA ** (-jnp.arange(0, QK_ROPE, 2, dtype=F32) / QK_ROPE)
    c_in, c_out, c_shapes = _cast_plan(casts, t_total // tm, lambda i: i)
    return pl.pallas_call(
        functools.partial(_rope_tab_kernel, len(casts)),
        grid=(t_total // tm,),
        in_specs=[pl.BlockSpec((1, tm), lambda i: (0, i)), _const_spec((QK_ROPE // 2, 1))] + c_in,
        out_specs=[pl.BlockSpec((tm, 2 * QK_ROPE), lambda i: (i, 0)),
                   pl.BlockSpec((2 * QK_ROPE, tm), lambda i: (0, i))] + c_out,
        out_shape=[jax.ShapeDtypeStruct((t_total, 2 * QK_ROPE), F32),
                   jax.ShapeDtypeStruct((2 * QK_ROPE, t_total), F32)] + c_shapes,
        compiler_params=_params("arbitrary"),
        name="rope_table",
    )(pos, inv_freq.reshape(QK_ROPE // 2, 1), *[c.w for c in casts])


def _rope(x2, tab):
    prod = x2 * tab
    return prod + pltpu.roll(prod, QK_ROPE, axis=1)


def _ffn_up(x, pre_g_ref, wg_ref, wu_ref, a_ref, r):
    n = _rms(x, pre_g_ref[...]).astype(BF16)
    for c in range(D_FF // FF_COLS):
        sl = slice(c * FF_COLS, (c + 1) * FF_COLS)
        g = jnp.dot(n, wg_ref[:, sl], preferred_element_type=F32)
        u = jnp.dot(n, wu_ref[:, sl], preferred_element_type=F32)
        a_ref[r, sl] = (g * jax.nn.sigmoid(g) * u).astype(BF16)


def _ffn_down(x, post_g_ref, wd_ref, a_ref, r):
    f = _dot_col_tiles(a_ref[r, :], wd_ref)
    return x + 0.5 * _rms(f, post_g_ref[...])


_NT = (((1,), (1,)), ((), ()))


def _kv_stage(xs, tiles, g_ref, w_c_ref, w_rope_ref, ga_ref, w_uk_ref, w_uv_t_ref, tab_ref,
              k_ref, vt_ref):
    ns = [_rms(x, g_ref[...]).astype(BF16) for x in xs]
    cs = [_rms(jnp.dot(n, w_c_ref[...], preferred_element_type=F32), ga_ref[...]).astype(BF16)
          for n in ns]
    ones = jnp.ones((BF16_ROWS, tiles[0].stop - tiles[0].start), BF16)
    for n, c, r in zip(ns, cs, tiles):
        k = _dot_col_tiles(c, w_uk_ref)
        v_t = lax.dot_general(w_uv_t_ref[...], c, _NT, preferred_element_type=F32)
        k2 = jnp.dot(n, w_rope_ref[...], preferred_element_type=F32)
        kr = _rope(k2, tab_ref[r, :]).astype(BF16)
        for hh in range(N_HEADS):
            k_ref[hh, r, :QK_NOPE] = k[:, hh * QK_NOPE:(hh + 1) * QK_NOPE].astype(BF16)
            k_ref[hh, r, QK_NOPE:] = kr
            vt_ref[hh * VT_ROWS:hh * VT_ROWS + V_DIM, r] = (
                v_t[hh * V_DIM:(hh + 1) * V_DIM].astype(BF16))
            vt_ref[hh * VT_ROWS + V_DIM:(hh + 1) * VT_ROWS, r] = ones


def _q_stage(xs, tiles, g_ref, w_dq_ref, gq_ref, w_uq_t_ref, tab_t_ref, qt_ref):
    ns = [_rms(x, g_ref[...]).astype(BF16) for x in xs]
    qns = [_rms(jnp.dot(n, w_dq_ref[...], preferred_element_type=F32), gq_ref[...]).astype(BF16)
           for n in ns]
    for qn, r in zip(qns, tiles):
        q_t = lax.dot_general(w_uq_t_ref[...], qn, _NT, preferred_element_type=F32)
        q_t = q_t * QK_SCALE_LOG2E
        tab_t = tab_t_ref[:, r]
        pad = jnp.zeros((HEAD_PAD - QK_NOPE - QK_ROPE, q_t.shape[1]), BF16)
        for hh in range(N_HEADS):
            base = hh * HEAD_PAD
            qt_ref[base:base + QK_NOPE, r] = q_t[base:base + QK_NOPE].astype(BF16)
            prod = q_t[base + QK_NOPE:base + HEAD_PAD] * tab_t
            rope_t = prod[:QK_ROPE] + prod[QK_ROPE:]
            qt_ref[base + QK_NOPE:base + QK_NOPE + QK_ROPE, r] = rope_t.astype(BF16)
            qt_ref[base + QK_NOPE + QK_ROPE:base + HEAD_PAD, r] = pad


def _ffn_kernel(n_cast, *refs):
    ((h_ref, pre_g_ref, post_g_ref, wg_ref, wu_ref, wd_ref), src, (o_ref,), dst,
     (a_ref,)) = _split(refs, 6, n_cast, 1, n_cast, 1)
    _cast_blocks(src, dst)
    tiles = _row_tiles(h_ref.shape[0], SUB_FFN)
    for r in tiles:
        _ffn_up(h_ref[r, :], pre_g_ref, wg_ref, wu_ref, a_ref, r)
    for r in tiles:
        o_ref[r, :] = _ffn_down(h_ref[r, :], post_g_ref, wd_ref, a_ref, r)


def _ffn_kv_kernel(h_ref, pre_g_ref, post_g_ref, wg_ref, wu_ref, wd_ref,
                   g_ref, w_c_ref, w_rope_ref, ga_ref, w_uk_ref, w_uv_t_ref, tab_ref,
                   o_ref, k_ref, vt_ref, a_ref):
    tiles = _row_tiles(h_ref.shape[0], SUB_FUSED)
    for r in tiles:
        _ffn_up(h_ref[r, :], pre_g_ref, wg_ref, wu_ref, a_ref, r)
    for r in tiles:
        o_ref[r, :] = _ffn_down(h_ref[r, :], post_g_ref, wd_ref, a_ref, r)
    _kv_stage([o_ref[r, :] for r in tiles], tiles, g_ref, w_c_ref, w_rope_ref, ga_ref,
              w_uk_ref, w_uv_t_ref, tab_ref, k_ref, vt_ref)


def _ffn_q_kernel(h_ref, pre_g_ref, post_g_ref, wg_ref, wu_ref, wd_ref,
                  g_ref, w_dq_ref, gq_ref, w_uq_t_ref, tab_t_ref, o_ref, qt_ref, a_ref):
    tiles = _row_tiles(h_ref.shape[0], SUB_FUSED)
    for r in tiles:
        _ffn_up(h_ref[r, :], pre_g_ref, wg_ref, wu_ref, a_ref, r)
    for r in tiles:
        o_ref[r, :] = _ffn_down(h_ref[r, :], post_g_ref, wd_ref, a_ref, r)
    _q_stage([o_ref[r, :] for r in tiles], tiles, g_ref, w_dq_ref, gq_ref, w_uq_t_ref, tab_t_ref,
             qt_ref)


def _oproj_ffn_kernel(h_ref, att_ref, w_o_ref, g_mix_ref, pre_g_ref, post_g_ref, wg_ref, wu_ref,
                      wd_ref, o_ref, a_ref, h1_ref):
    tiles = _row_tiles(h_ref.shape[0], SUB_FUSED)
    for r in tiles:
        m = _dot_col_tiles(att_ref[r, :], w_o_ref)
        h1_ref[r, :] = h_ref[r, :] + _rms(m, g_mix_ref[...])
    for r in tiles:
        _ffn_up(h1_ref[r, :], pre_g_ref, wg_ref, wu_ref, a_ref, r)
    for r in tiles:
        o_ref[r, :] = _ffn_down(h1_ref[r, :], post_g_ref, wd_ref, a_ref, r)


def _ffn_weight_specs(layer, j):
    gain = pl.BlockSpec((None, None, 1, D_MODEL), lambda *_: (layer, j, 0, 0),
                        pipeline_mode=pl.Buffered(1))
    return [gain, gain, _const_spec((D_MODEL, D_FF)), _const_spec((D_MODEL, D_FF)),
            _const_spec((D_MODEL // W_COLS, D_FF, W_COLS))]


def _ffn(h, gains, w, layer, j, casts):
    t_total = h.shape[0]
    tm = TM_FFN
    c_in, c_out, c_shapes = _cast_plan(casts, t_total // tm, lambda i: i)
    return pl.pallas_call(
        functools.partial(_ffn_kernel, len(casts)),
        grid=(t_total // tm,),
        in_specs=[_row_spec(tm, D_MODEL)] + _ffn_weight_specs(layer, j) + c_in,
        out_specs=[_row_spec(tm, D_MODEL)] + c_out,
        out_shape=[jax.ShapeDtypeStruct((t_total, D_MODEL), F32)] + c_shapes,
        scratch_shapes=[pltpu.VMEM((tm, D_FF), BF16)],
        compiler_params=_params("arbitrary"),
        name="ffn",
    )(h, *gains, *w, *[c.w for c in casts])


def _seq_cols_spec(rows, tm, per_b):
    return pl.BlockSpec((None, rows, tm), lambda i: (i // per_b, 0, i % per_b))


def _ffn_kv(h, gains, w, layer, j, batch, seq, g, w_c, w_rope, ga, w_uk, w_uv_t, tab):
    t_total = h.shape[0]
    tm = TM_FUSED
    return pl.pallas_call(
        _ffn_kv_kernel,
        grid=(t_total // tm,),
        in_specs=[_row_spec(tm, D_MODEL)] + _ffn_weight_specs(layer, j) + [
            _const_spec((1, D_MODEL)), _const_spec((D_MODEL, KV_RANK)),
            _const_spec((D_MODEL, 2 * QK_ROPE)), _const_spec((1, KV_RANK)),
            _const_spec((N_HEADS * QK_NOPE // W_COLS, KV_RANK, W_COLS)),
            _const_spec((N_HEADS * V_DIM, KV_RANK)),
            _row_spec(tm, 2 * QK_ROPE)],
        out_specs=[_row_spec(tm, D_MODEL),
                   pl.BlockSpec((None, N_HEADS, tm, HEAD_PAD),
                                lambda i: (i // (seq // tm), 0, i % (seq // tm), 0)),
                   _seq_cols_spec(N_HEADS * VT_ROWS, tm, seq // tm)],
        out_shape=[jax.ShapeDtypeStruct((t_total, D_MODEL), F32),
                   jax.ShapeDtypeStruct((batch, N_HEADS, seq, HEAD_PAD), BF16),
                   jax.ShapeDtypeStruct((batch, N_HEADS * VT_ROWS, seq), BF16)],
        scratch_shapes=[pltpu.VMEM((tm, D_FF), BF16)],
        compiler_params=_params("parallel"),
        name="ffn_kv",
    )(h, *gains, *w, g.reshape(1, D_MODEL), w_c, w_rope, ga.reshape(1, KV_RANK), w_uk, w_uv_t, tab)


def _ffn_q(h, gains, w, layer, j, batch, seq, g, w_dq, gq, w_uq_t, tab_t):
    t_total = h.shape[0]
    tm = TM_FUSED
    return pl.pallas_call(
        _ffn_q_kernel,
        grid=(t_total // tm,),
        in_specs=[_row_spec(tm, D_MODEL)] + _ffn_weight_specs(layer, j) + [
            _const_spec((1, D_MODEL)), _const_spec((D_MODEL, Q_RANK)), _const_spec((1, Q_RANK)),
            _const_spec((N_HEADS * HEAD_PAD, Q_RANK)),
            pl.BlockSpec((2 * QK_ROPE, tm), lambda i: (0, i))],
        out_specs=[_row_spec(tm, D_MODEL), _seq_cols_spec(N_HEADS * HEAD_PAD, tm, seq // tm)],
        out_shape=[jax.ShapeDtypeStruct((t_total, D_MODEL), F32),
                   jax.ShapeDtypeStruct((batch, N_HEADS * HEAD_PAD, seq), BF16)],
        scratch_shapes=[pltpu.VMEM((tm, D_FF), BF16)],
        compiler_params=_params("parallel"),
        name="ffn_q",
    )(h, *gains, *w, g.reshape(1, D_MODEL), w_dq, gq.reshape(1, Q_RANK), w_uq_t, tab_t)


def _oproj_ffn(h, att, w_o, g_mix, gains, w, layer, j):
    t_total = h.shape[0]
    tm = TM_FUSED
    return pl.pallas_call(
        _oproj_ffn_kernel,
        grid=(t_total // tm,),
        in_specs=[_row_spec(tm, D_MODEL), _row_spec(tm, N_HEADS * V_DIM),
                  _const_spec((D_MODEL // W_COLS, N_HEADS * V_DIM, W_COLS)),
                  _const_spec((1, D_MODEL))] + _ffn_weight_specs(layer, j),
        out_specs=_row_spec(tm, D_MODEL),
        out_shape=jax.ShapeDtypeStruct((t_total, D_MODEL), F32),
        scratch_shapes=[pltpu.VMEM((tm, D_FF), BF16), pltpu.VMEM((tm, D_MODEL), F32)],
        compiler_params=_params("parallel"),
        name="oproj_ffn",
    )(h, att, w_o, g_mix.reshape(1, D_MODEL), *gains, *w)


_GELU_A = -2.0 * 0.7978845608028654 * 1.4426950408889634
_GELU_B = _GELU_A * 0.044715


def _gelu(x):
    return x / (1.0 + jnp.exp2(x * (_GELU_A + _GELU_B * (x * x))))


def _gmlp_kernel(n_cast, *refs):
    ((h_ref, pre_g_ref, post_g_ref, w_in_ref, ln_g_ref, ln_b_ref, w_s_ref, b_st_ref, w_out_ref),
     src, (o_ref,), dst, (wm_ref, bias_ref, n_ref, v_ref, vn_ref, gated_ref)) = _split(
         refs, 9, n_cast, 1, n_cast, 6)
    _cast_blocks(src, dst)

    @pl.when(pl.program_id(0) == 0)
    def _():
        t_idx = lax.broadcasted_iota(jnp.int32, (CHUNK, CHUNK), 0)
        c_idx = lax.broadcasted_iota(jnp.int32, (CHUNK, CHUNK), 1)
        for g in range(GMLP_GROUPS):
            wm_ref[g] = jnp.where(c_idx <= t_idx, w_s_ref[g], 0.0).astype(BF16)
            bias_ref[g] = jnp.broadcast_to(b_st_ref[:, g:g + 1], (CHUNK, CHUNK))

    tiles = _row_tiles(h_ref.shape[0], SUB_GMLP)
    for r in tiles:
        n_ref[r, :] = _rms(h_ref[r, :], pre_g_ref[...]).astype(BF16)
    n_half = GMLP_HALF // W_COLS
    for r in tiles:
        for c in range(n_half):
            z = jnp.dot(n_ref[r, :], w_in_ref[n_half + c], preferred_element_type=F32)
            v_ref[r, c * W_COLS:(c + 1) * W_COLS] = _gelu(z)
    for r in tiles:
        v = v_ref[r, :]
        mu = jnp.mean(v, axis=-1, keepdims=True)
        xc = v - mu
        var = jnp.mean(xc * xc, axis=-1, keepdims=True)
        vn = xc * lax.rsqrt(var + LN_EPS) * ln_g_ref[...] + ln_b_ref[...]
        vn_ref[r, :] = vn.astype(BF16)
    for r in tiles:
        for gp in range(n_half):
            u = _gelu(jnp.dot(n_ref[r, :], w_in_ref[gp], preferred_element_type=F32))
            for gg in range(2):
                g = 2 * gp + gg
                gl = slice(g * CHUNK, (g + 1) * CHUNK)
                for ci in range(SUB_GMLP // CHUNK):
                    rc = slice(r.start + ci * CHUNK, r.start + (ci + 1) * CHUNK)
                    sv = jnp.dot(wm_ref[g], vn_ref[rc, gl], preferred_element_type=F32) + bias_ref[g]
                    u_blk = u[ci * CHUNK:(ci + 1) * CHUNK, gg * CHUNK:(gg + 1) * CHUNK]
                    gated_ref[rc, gl] = (u_blk * sv).astype(BF16)
    for r in tiles:
        m = _dot_col_tiles(gated_ref[r, :], w_out_ref)
        o_ref[r, :] = h_ref[r, :] + _rms(m, post_g_ref[...])


def _gmlp(h, pre_g, post_g, w_in, ln_g, ln_b, w_s, b_s, w_out, casts):
    t_total = h.shape[0]
    tm = TM_GMLP
    c_in, c_out, c_shapes = _cast_plan(casts, t_total // tm, lambda i: i)
    return pl.pallas_call(
        functools.partial(_gmlp_kernel, len(casts)),
        grid=(t_total // tm,),
        in_specs=[_row_spec(tm, D_MODEL), _const_spec((1, D_MODEL)), _const_spec((1, D_MODEL)),
                  _const_spec((2 * GMLP_HALF // W_COLS, D_MODEL, W_COLS)),
                  _const_spec((1, GMLP_HALF)), _const_spec((1, GMLP_HALF)),
                  _const_spec((GMLP_GROUPS, CHUNK, CHUNK)), _const_spec((CHUNK, GMLP_GROUPS)),
                  _const_spec((D_MODEL // W_COLS, GMLP_HALF, W_COLS))] + c_in,
        out_specs=[_row_spec(tm, D_MODEL)] + c_out,
        out_shape=[jax.ShapeDtypeStruct((t_total, D_MODEL), F32)] + c_shapes,
        scratch_shapes=[pltpu.VMEM((GMLP_GROUPS, CHUNK, CHUNK), BF16),
                        pltpu.VMEM((GMLP_GROUPS, CHUNK, CHUNK), F32),
                        pltpu.VMEM((tm, D_MODEL), BF16),
                        pltpu.VMEM((tm, GMLP_HALF), F32), pltpu.VMEM((tm, GMLP_HALF), BF16),
                        pltpu.VMEM((tm, GMLP_HALF), BF16)],
        compiler_params=_params("arbitrary"),
        name="gmlp",
    )(h, pre_g.reshape(1, D_MODEL), post_g.reshape(1, D_MODEL), w_in,
      ln_g.reshape(1, GMLP_HALF), ln_b.reshape(1, GMLP_HALF), w_s, b_s.T, w_out,
      *[c.w for c in casts])


def _attn_kernel(n_cast, *refs):
    ((qt_ref, k_ref, vt_ref), src, (o_ref,), dst) = _split(refs, 3, n_cast, 1, n_cast)
    _cast_blocks(src, dst)
    seq = k_ref.shape[1]
    n_blk = seq // TQ
    k_idx = lax.broadcasted_iota(jnp.int32, (TQ, TQ), 0)
    q_idx = lax.broadcasted_iota(jnp.int32, (TQ, TQ), 1)
    causal = k_idx <= q_idx
    vts = [vt_ref[hd * VT_ROWS:(hd + 1) * VT_ROWS, :] for hd in range(HEADS_PER_STEP)]

    def scores(hd, qi):
        return jnp.dot(k_ref[hd, :(qi + 1) * TQ, :],
                       qt_ref[hd * HEAD_PAD:(hd + 1) * HEAD_PAD, qi * TQ:(qi + 1) * TQ],
                       preferred_element_type=F32)

    tasks = [(hd, qi) for qi in range(n_blk) for hd in range(HEADS_PER_STEP)]
    pending = [scores(*t) for t in tasks[:HEADS_PER_STEP]]
    for k, (hd, qi) in enumerate(tasks):
        q0, q1 = qi * TQ, (qi + 1) * TQ
        s = pending.pop(0)
        if k + HEADS_PER_STEP < len(tasks):
            pending.append(scores(*tasks[k + HEADS_PER_STEP]))
        s_dg = jnp.where(causal, s[q0:q1, :], NEG_INF)
        m = jnp.max(s_dg, axis=0, keepdims=True)
        if qi > 0:
            m = jnp.maximum(m, jnp.max(s[:q0, :], axis=0, keepdims=True))
        p = jnp.exp2(s_dg - m).astype(BF16)
        if qi > 0:
            p = jnp.concatenate([jnp.exp2(s[:q0, :] - m).astype(BF16), p], axis=0)
        acc = jnp.dot(vts[hd][:, :q1], p, preferred_element_type=F32)
        o_t = acc[:V_DIM] / acc[V_DIM:V_DIM + 1]
        o_ref[q0:q1, hd * V_DIM:(hd + 1) * V_DIM] = o_t.T.astype(BF16)


def _attention(qt, k, vt, batch, seq, casts):
    hps = HEADS_PER_STEP
    groups = N_HEADS // hps
    c_in, c_out, c_shapes = _cast_plan(casts, batch * groups, lambda b, h: b * groups + h)
    out, *cast = pl.pallas_call(
        functools.partial(_attn_kernel, len(casts)),
        grid=(batch, groups),
        in_specs=[pl.BlockSpec((None, hps * HEAD_PAD, seq), lambda b, h: (b, h, 0)),
                  pl.BlockSpec((None, hps, seq, HEAD_PAD), lambda b, h: (b, h, 0, 0)),
                  pl.BlockSpec((None, hps * VT_ROWS, seq), lambda b, h: (b, h, 0))] + c_in,
        out_specs=[pl.BlockSpec((None, seq, hps * V_DIM), lambda b, h: (b, 0, h))] + c_out,
        out_shape=[jax.ShapeDtypeStruct((batch, seq, N_HEADS * V_DIM), BF16)] + c_shapes,
        compiler_params=_params("arbitrary", "arbitrary"),
        name="mla_attention",
    )(qt, k, vt, *[c.w for c in casts])
    return out.reshape(batch * seq, N_HEADS * V_DIM), cast


def _swap_halves(w):
    half = w.shape[-1] // 2
    return jnp.concatenate([w[..., half:], w[..., :half]], axis=-1)


def _kv_weights(w_dkv, w_ukv):
    w_c = w_dkv[:, :KV_RANK].astype(BF16)
    w_r = w_dkv[:, KV_RANK:]
    w_rope = jnp.concatenate([w_r, _swap_halves(w_r)], axis=1).astype(BF16)
    w4 = w_ukv.reshape(KV_RANK, N_HEADS, QK_NOPE + V_DIM)
    w_uk = _col_tiles(w4[:, :, :QK_NOPE].reshape(KV_RANK, N_HEADS * QK_NOPE))
    w_uv_t = w4[:, :, QK_NOPE:].reshape(KV_RANK, N_HEADS * V_DIM).T.astype(BF16)
    return w_c, w_rope, w_uk, w_uv_t


def _q_weights(w_uq):
    w4 = w_uq.reshape(Q_RANK, N_HEADS, QK_NOPE + QK_ROPE)
    w_r = w4[:, :, QK_NOPE:]
    w = jnp.concatenate([w4[:, :, :QK_NOPE], w_r, _swap_halves(w_r)], axis=-1)
    return w.reshape(Q_RANK, N_HEADS * HEAD_PAD).T.astype(BF16)


def kernel(x, positions, ffn_pre_g, ffn_post_g, ffn_w_gate, ffn_w_up, ffn_w_down, mix_pre_g, mix_post_g, gmlp_w_in, gmlp_ln_g, gmlp_ln_b, gmlp_w_s, gmlp_b_s, gmlp_w_out, kv_norm_g, w_dkv, kv_a_norm_g, w_ukv, mla_w_dq, mla_q_norm_g, mla_w_uq, mla_w_o):
    batch, seq, _ = x.shape
    h = x.reshape(batch * seq, D_MODEL)
    depth = ffn_pre_g.shape[0]
    gains = (ffn_pre_g.reshape(depth, 2, 1, D_MODEL), ffn_post_g.reshape(depth, 2, 1, D_MODEL))

    def ffn_casts(layer, j):
        return [_Cast(ffn_w_gate, (layer, j), False), _Cast(ffn_w_up, (layer, j), False),
                _Cast(ffn_w_down, (layer, j), True)]

    tab, tab_t, *w00 = _rope_table(positions, ffn_casts(0, 0))
    h, w_in, w_out = _ffn(h, gains, w00, 0, 0,
                          [_Cast(gmlp_w_in, (0,), True), _Cast(gmlp_w_out, (0,), True)])
    h, *w_ffn = _gmlp(h, mix_pre_g[0], mix_post_g[0], w_in, gmlp_ln_g[0], gmlp_ln_b[0],
                      gmlp_w_s[0], gmlp_b_s[0], w_out, ffn_casts(0, 1) + ffn_casts(1, 0))
    w01, w10 = w_ffn[:3], w_ffn[3:]
    w_c, w_rope, w_uk, w_uv_t = _kv_weights(w_dkv, w_ukv)
    h, k, vt = _ffn_kv(h, gains, w01, 0, 1, batch, seq, kv_norm_g, w_c, w_rope,
                       kv_a_norm_g, w_uk, w_uv_t, tab)
    h, qt = _ffn_q(h, gains, w10, 1, 0, batch, seq, mix_pre_g[1], mla_w_dq[0].astype(BF16),
                   mla_q_norm_g[0], _q_weights(mla_w_uq[0]), tab_t)
    att, w11 = _attention(qt, k, vt, batch, seq, ffn_casts(1, 1))
    h = _oproj_ffn(h, att, _col_tiles(mla_w_o[0]), mix_post_g[1], gains, w11, 1, 1)
    return h.reshape(batch, seq, D_MODEL)
```

```python
import functools
from typing import NamedTuple

import jax
import jax.numpy as jnp
from jax import lax
from jax.experimental import pallas as pl
from jax.experimental.pallas import tpu as pltpu

D_MODEL = 1024
D_FF = 2816
CHUNK = 128
GMLP_HALF = 2 * D_MODEL
GMLP_GROUPS = 16
N_HEADS = 8
QK_NOPE = 128
QK_ROPE = 64
V_DIM = 128
KV_RANK = 256
Q_RANK = 512
ROPE_THETA = 10000.0
RMS_EPS = 1e-6
LN_EPS = 1e-5
NEG_INF = -1e30
QK_SCALE_LOG2E = (QK_NOPE + QK_ROPE) ** -0.5 * 1.4426950408889634

HEAD_PAD = 256
FF_COLS = 256
W_COLS = 256
TM_FFN = 1024
SUB_FFN = 256
TM_GMLP = 1024
SUB_GMLP = 256
TM_FUSED = 1024
SUB_FUSED = 256
TM_ROPE = 2048
TQ = 256
HEADS_PER_STEP = 4
BF16_ROWS = 16
VT_ROWS = V_DIM + BF16_ROWS
VMEM_LIMIT = 56 * 1024 * 1024

BF16 = jnp.bfloat16
F32 = jnp.float32


def _rms(x, g):
    return x * lax.rsqrt(jnp.mean(x * x, axis=-1, keepdims=True) + RMS_EPS) * g


def _const_spec(shape):
    return pl.BlockSpec(shape, lambda *_: (0,) * len(shape), pipeline_mode=pl.Buffered(1))


def _row_spec(tm, width):
    return pl.BlockSpec((tm, width), lambda i: (i, 0))


def _row_tiles(rows, sub):
    return [slice(i * sub, (i + 1) * sub) for i in range(rows // sub)]


def _staggered(tiles, first, second):
    first(tiles[0])
    for prev, r in zip(tiles, tiles[1:]):
        first(r)
        second(prev)
    second(tiles[-1])


def _col_tiles(w):
    k, n = w.shape
    return jnp.swapaxes(w.reshape(k, n // W_COLS, W_COLS), 0, 1).astype(BF16)


def _dot_col_tiles(x, w_ref):
    return jnp.concatenate([jnp.dot(x, w_ref[c], preferred_element_type=F32)
                            for c in range(w_ref.shape[0])], axis=1)


def _params(*semantics):
    return pltpu.CompilerParams(dimension_semantics=semantics, vmem_limit_bytes=VMEM_LIMIT)


class _Cast(NamedTuple):
    w: jax.Array
    lead: tuple
    col_tiled: bool


def _cast_plan(jobs, n_steps, step_of):
    in_specs, out_specs, out_shapes = [], [], []
    for job in jobs:
        k, n = job.w.shape[-2:]
        chunks = max(c for c in range(1, n_steps + 1) if k % (BF16_ROWS * c) == 0)
        rows = k // chunks

        def chunk(*idx, chunks=chunks):
            return jnp.minimum(step_of(*idx), chunks - 1)

        in_specs.append(pl.BlockSpec((None,) * len(job.lead) + (rows, n),
                                     lambda *idx, lead=job.lead, chunk=chunk: lead + (chunk(*idx), 0)))
        if job.col_tiled:
            out_specs.append(pl.BlockSpec((n // W_COLS, rows, W_COLS),
                                          lambda *idx, chunk=chunk: (0, chunk(*idx), 0)))
            out_shapes.append(jax.ShapeDtypeStruct((n // W_COLS, k, W_COLS), BF16))
        else:
            out_specs.append(pl.BlockSpec((rows, n), lambda *idx, chunk=chunk: (chunk(*idx), 0)))
            out_shapes.append(jax.ShapeDtypeStruct((k, n), BF16))
    return in_specs, out_specs, out_shapes


def _cast_blocks(src_refs, dst_refs):
    for src, dst in zip(src_refs, dst_refs):
        if len(dst.shape) == 3:
            for c in range(dst.shape[0]):
                dst[c] = src[:, c * W_COLS:(c + 1) * W_COLS].astype(BF16)
        else:
            dst[...] = src[...].astype(BF16)


def _split(refs, *counts):
    out, i = [], 0
    for c in counts:
        out.append(refs[i:i + c])
        i += c
    assert i == len(refs)
    return out


def _rope_tab_kernel(n_cast, *refs):
    (pos_ref, invf_ref), src, (tab_ref, tab_t_ref), dst = _split(refs, 2, n_cast, 2, n_cast)
    _cast_blocks(src, dst)
    ang = invf_ref[...] * pos_ref[...]
    c = jnp.cos(ang)
    s = jnp.sin(ang)
    t = jnp.concatenate([c, c, -s, s], axis=0)
    tab_t_ref[...] = t
    tab_ref[...] = t.T


def _rope_table(positions, casts):
    t_total = positions.size
    tm = TM_ROPE
    pos = positions.reshape(1, t_total).astype(F32)
    inv_freq = ROPE_THETA ** (-jnp.arange(0, QK_ROPE, 2, dtype=F32) / QK_ROPE)
    c_in, c_out, c_shapes = _cast_plan(casts, t_total // tm, lambda i: i)
    return pl.pallas_call(
        functools.partial(_rope_tab_kernel, len(casts)),
        grid=(t_total // tm,),
        in_specs=[pl.BlockSpec((1, tm), lambda i: (0, i)), _const_spec((QK_ROPE // 2, 1))] + c_in,
        out_specs=[pl.BlockSpec((tm, 2 * QK_ROPE), lambda i: (i, 0)),
                   pl.BlockSpec((2 * QK_ROPE, tm), lambda i: (0, i))] + c_out,
        out_shape=[jax.ShapeDtypeStruct((t_total, 2 * QK_ROPE), F32),
                   jax.ShapeDtypeStruct((2 * QK_ROPE, t_total), F32)] + c_shapes,
        compiler_params=_params("arbitrary"),
        name="rope_table",
    )(pos, inv_freq.reshape(QK_ROPE // 2, 1), *[c.w for c in casts])


def _rope(x2, tab):
    prod = x2 * tab
    return prod + pltpu.roll(prod, QK_ROPE, axis=1)


def _ffn_up(x, pre_g_ref, wg_ref, wu_ref, a_ref, r):
    n = _rms(x, pre_g_ref[...]).astype(BF16)
    for c in range(D_FF // FF_COLS):
        sl = slice(c * FF_COLS, (c + 1) * FF_COLS)
        g = jnp.dot(n, wg_ref[:, sl], preferred_element_type=F32)
        u = jnp.dot(n, wu_ref[:, sl], preferred_element_type=F32)
        a_ref[r, sl] = (g * jax.nn.sigmoid(g) * u).astype(BF16)


def _ffn_down(x, post_g_ref, wd_ref, a_ref, r):
    f = _dot_col_tiles(a_ref[r, :], wd_ref)
    return x + 0.5 * _rms(f, post_g_ref[...])


_NT = (((1,), (1,)), ((), ()))


def _kv_stage(xs, tiles, g_ref, w_c_ref, w_rope_ref, ga_ref, w_uk_ref, w_uv_t_ref, tab_ref,
              k_ref, vt_ref):
    ns = [_rms(x, g_ref[...]).astype(BF16) for x in xs]
    cs = [_rms(jnp.dot(n, w_c_ref[...], preferred_element_type=F32), ga_ref[...]).astype(BF16)
          for n in ns]
    ones = jnp.ones((BF16_ROWS, tiles[0].stop - tiles[0].start), BF16)
    for n, c, r in zip(ns, cs, tiles):
        k = _dot_col_tiles(c, w_uk_ref)
        v_t = lax.dot_general(w_uv_t_ref[...], c, _NT, preferred_element_type=F32)
        k2 = jnp.dot(n, w_rope_ref[...], preferred_element_type=F32)
        kr = _rope(k2, tab_ref[r, :]).astype(BF16)
        for hh in range(N_HEADS):
            k_ref[hh, r, :QK_NOPE] = k[:, hh * QK_NOPE:(hh + 1) * QK_NOPE].astype(BF16)
            k_ref[hh, r, QK_NOPE:] = kr
            vt_ref[hh * VT_ROWS:hh * VT_ROWS + V_DIM, r] = (
                v_t[hh * V_DIM:(hh + 1) * V_DIM].astype(BF16))
            vt_ref[hh * VT_ROWS + V_DIM:(hh + 1) * VT_ROWS, r] = ones


def _q_stage(xs, tiles, g_ref, w_dq_ref, gq_ref, w_uq_t_ref, tab_t_ref, qt_ref):
    ns = [_rms(x, g_ref[...]).astype(BF16) for x in xs]
    qns = [_rms(jnp.dot(n, w_dq_ref[...], preferred_element_type=F32), gq_ref[...]).astype(BF16)
           for n in ns]
    for qn, r in zip(qns, tiles):
        q_t = lax.dot_general(w_uq_t_ref[...], qn, _NT, preferred_element_type=F32)
        q_t = q_t * QK_SCALE_LOG2E
        tab_t = tab_t_ref[:, r]
        pad = jnp.zeros((HEAD_PAD - QK_NOPE - QK_ROPE, q_t.shape[1]), BF16)
        for hh in range(N_HEADS):
            base = hh * HEAD_PAD
            qt_ref[base:base + QK_NOPE, r] = q_t[base:base + QK_NOPE].astype(BF16)
            prod = q_t[base + QK_NOPE:base + HEAD_PAD] * tab_t
            rope_t = prod[:QK_ROPE] + prod[QK_ROPE:]
            qt_ref[base + QK_NOPE:base + QK_NOPE + QK_ROPE, r] = rope_t.astype(BF16)
            qt_ref[base + QK_NOPE + QK_ROPE:base + HEAD_PAD, r] = pad


def _ffn_kernel(n_cast, *refs):
    ((h_ref, pre_g_ref, post_g_ref, wg_ref, wu_ref, wd_ref), src, (o_ref,), dst,
     (a_ref,)) = _split(refs, 6, n_cast, 1, n_cast, 1)
    _cast_blocks(src, dst)
    tiles = _row_tiles(h_ref.shape[0], SUB_FFN)
    for r in tiles:
        _ffn_up(h_ref[r, :], pre_g_ref, wg_ref, wu_ref, a_ref, r)
    for r in tiles:
        o_ref[r, :] = _ffn_down(h_ref[r, :], post_g_ref, wd_ref, a_ref, r)


def _ffn_kv_kernel(h_ref, pre_g_ref, post_g_ref, wg_ref, wu_ref, wd_ref,
                   g_ref, w_c_ref, w_rope_ref, ga_ref, w_uk_ref, w_uv_t_ref, tab_ref,
                   o_ref, k_ref, vt_ref, a_ref):
    tiles = _row_tiles(h_ref.shape[0], SUB_FUSED)
    for r in tiles:
        _ffn_up(h_ref[r, :], pre_g_ref, wg_ref, wu_ref, a_ref, r)
    for r in tiles:
        o_ref[r, :] = _ffn_down(h_ref[r, :], post_g_ref, wd_ref, a_ref, r)
    _kv_stage([o_ref[r, :] for r in tiles], tiles, g_ref, w_c_ref, w_rope_ref, ga_ref,
              w_uk_ref, w_uv_t_ref, tab_ref, k_ref, vt_ref)


def _ffn_q_kernel(h_ref, pre_g_ref, post_g_ref, wg_ref, wu_ref, wd_ref,
                  g_ref, w_dq_ref, gq_ref, w_uq_t_ref, tab_t_ref, o_ref, qt_ref, a_ref):
    tiles = _row_tiles(h_ref.shape[0], SUB_FUSED)
    for r in tiles:
        _ffn_up(h_ref[r, :], pre_g_ref, wg_ref, wu_ref, a_ref, r)
    for r in tiles:
        o_ref[r, :] = _ffn_down(h_ref[r, :], post_g_ref, wd_ref, a_ref, r)
    _q_stage([o_ref[r, :] for r in tiles], tiles, g_ref, w_dq_ref, gq_ref, w_uq_t_ref, tab_t_ref,
             qt_ref)


def _oproj_ffn_kernel(h_ref, att_ref, w_o_ref, g_mix_ref, pre_g_ref, post_g_ref, wg_ref, wu_ref,
                      wd_ref, o_ref, a_ref, h1_ref):
    tiles = _row_tiles(h_ref.shape[0], SUB_FUSED)
    def project(r):
        m = _dot_col_tiles(att_ref[r, :], w_o_ref)
        h1_ref[r, :] = h_ref[r, :] + _rms(m, g_mix_ref[...])

    def up(r):
        _ffn_up(h1_ref[r, :], pre_g_ref, wg_ref, wu_ref, a_ref, r)

    _staggered(tiles, project, up)
    for r in tiles:
        o_ref[r, :] = _ffn_down(h1_ref[r, :], post_g_ref, wd_ref, a_ref, r)


def _ffn_weight_specs(layer, j):
    gain = pl.BlockSpec((None, None, 1, D_MODEL), lambda *_: (layer, j, 0, 0),
                        pipeline_mode=pl.Buffered(1))
    return [gain, gain, _const_spec((D_MODEL, D_FF)), _const_spec((D_MODEL, D_FF)),
            _const_spec((D_MODEL // W_COLS, D_FF, W_COLS))]


def _ffn(h, gains, w, layer, j, casts):
    t_total = h.shape[0]
    tm = TM_FFN
    c_in, c_out, c_shapes = _cast_plan(casts, t_total // tm, lambda i: i)
    return pl.pallas_call(
        functools.partial(_ffn_kernel, len(casts)),
        grid=(t_total // tm,),
        in_specs=[_row_spec(tm, D_MODEL)] + _ffn_weight_specs(layer, j) + c_in,
        out_specs=[_row_spec(tm, D_MODEL)] + c_out,
        out_shape=[jax.ShapeDtypeStruct((t_total, D_MODEL), F32)] + c_shapes,
        scratch_shapes=[pltpu.VMEM((tm, D_FF), BF16)],
        compiler_params=_params("arbitrary"),
        name="ffn",
    )(h, *gains, *w, *[c.w for c in casts])


def _seq_cols_spec(rows, tm, per_b):
    return pl.BlockSpec((None, rows, tm), lambda i: (i // per_b, 0, i % per_b))


def _ffn_kv(h, gains, w, layer, j, batch, seq, g, w_c, w_rope, ga, w_uk, w_uv_t, tab):
    t_total = h.shape[0]
    tm = TM_FUSED
    return pl.pallas_call(
        _ffn_kv_kernel,
        grid=(t_total // tm,),
        in_specs=[_row_spec(tm, D_MODEL)] + _ffn_weight_specs(layer, j) + [
            _const_spec((1, D_MODEL)), _const_spec((D_MODEL, KV_RANK)),
            _const_spec((D_MODEL, 2 * QK_ROPE)), _const_spec((1, KV_RANK)),
            _const_spec((N_HEADS * QK_NOPE // W_COLS, KV_RANK, W_COLS)),
            _const_spec((N_HEADS * V_DIM, KV_RANK)),
            _row_spec(tm, 2 * QK_ROPE)],
        out_specs=[_row_spec(tm, D_MODEL),
                   pl.BlockSpec((None, N_HEADS, tm, HEAD_PAD),
                                lambda i: (i // (seq // tm), 0, i % (seq // tm), 0)),
                   _seq_cols_spec(N_HEADS * VT_ROWS, tm, seq // tm)],
        out_shape=[jax.ShapeDtypeStruct((t_total, D_MODEL), F32),
                   jax.ShapeDtypeStruct((batch, N_HEADS, seq, HEAD_PAD), BF16),
                   jax.ShapeDtypeStruct((batch, N_HEADS * VT_ROWS, seq), BF16)],
        scratch_shapes=[pltpu.VMEM((tm, D_FF), BF16)],
        compiler_params=_params("parallel"),
        name="ffn_kv",
    )(h, *gains, *w, g.reshape(1, D_MODEL), w_c, w_rope, ga.reshape(1, KV_RANK), w_uk, w_uv_t, tab)


def _ffn_q(h, gains, w, layer, j, batch, seq, g, w_dq, gq, w_uq_t, tab_t):
    t_total = h.shape[0]
    tm = TM_FUSED
    return pl.pallas_call(
        _ffn_q_kernel,
        grid=(t_total // tm,),
        in_specs=[_row_spec(tm, D_MODEL)] + _ffn_weight_specs(layer, j) + [
            _const_spec((1, D_MODEL)), _const_spec((D_MODEL, Q_RANK)), _const_spec((1, Q_RANK)),
            _const_spec((N_HEADS * HEAD_PAD, Q_RANK)),
            pl.BlockSpec((2 * QK_ROPE, tm), lambda i: (0, i))],
        out_specs=[_row_spec(tm, D_MODEL), _seq_cols_spec(N_HEADS * HEAD_PAD, tm, seq // tm)],
        out_shape=[jax.ShapeDtypeStruct((t_total, D_MODEL), F32),
                   jax.ShapeDtypeStruct((batch, N_HEADS * HEAD_PAD, seq), BF16)],
        scratch_shapes=[pltpu.VMEM((tm, D_FF), BF16)],
        compiler_params=_params("parallel"),
        name="ffn_q",
    )(h, *gains, *w, g.reshape(1, D_MODEL), w_dq, gq.reshape(1, Q_RANK), w_uq_t, tab_t)


def _oproj_ffn(h, att, w_o, g_mix, gains, w, layer, j):
    t_total = h.shape[0]
    tm = TM_FUSED
    return pl.pallas_call(
        _oproj_ffn_kernel,
        grid=(t_total // tm,),
        in_specs=[_row_spec(tm, D_MODEL), _row_spec(tm, N_HEADS * V_DIM),
                  _const_spec((D_MODEL // W_COLS, N_HEADS * V_DIM, W_COLS)),
                  _const_spec((1, D_MODEL))] + _ffn_weight_specs(layer, j),
        out_specs=_row_spec(tm, D_MODEL),
        out_shape=jax.ShapeDtypeStruct((t_total, D_MODEL), F32),
        scratch_shapes=[pltpu.VMEM((tm, D_FF), BF16), pltpu.VMEM((tm, D_MODEL), F32)],
        compiler_params=_params("parallel"),
        name="oproj_ffn",
    )(h, att, w_o, g_mix.reshape(1, D_MODEL), *gains, *w)


_GELU_A = -2.0 * 0.7978845608028654 * 1.4426950408889634
_GELU_B = _GELU_A * 0.044715


def _gelu(x):
    return x / (1.0 + jnp.exp2(x * (_GELU_A + _GELU_B * (x * x))))


def _gmlp_kernel(n_cast, *refs):
    ((h_ref, pre_g_ref, post_g_ref, w_in_ref, ln_g_ref, ln_b_ref, w_s_ref, b_st_ref, w_out_ref),
     src, (o_ref,), dst, (wm_ref, bias_ref, n_ref, v_ref, vn_ref, gated_ref)) = _split(
         refs, 9, n_cast, 1, n_cast, 6)
    _cast_blocks(src, dst)

    @pl.when(pl.program_id(0) == 0)
    def _():
        t_idx = lax.broadcasted_iota(jnp.int32, (CHUNK, CHUNK), 0)
        c_idx = lax.broadcasted_iota(jnp.int32, (CHUNK, CHUNK), 1)
        for g in range(GMLP_GROUPS):
            wm_ref[g] = jnp.where(c_idx <= t_idx, w_s_ref[g], 0.0).astype(BF16)
            bias_ref[g] = jnp.broadcast_to(b_st_ref[:, g:g + 1], (CHUNK, CHUNK))

    tiles = _row_tiles(h_ref.shape[0], SUB_GMLP)
    n_half = GMLP_HALF // W_COLS

    def v_stage(r):
        n_ref[r, :] = _rms(h_ref[r, :], pre_g_ref[...]).astype(BF16)
        for c in range(n_half):
            z = jnp.dot(n_ref[r, :], w_in_ref[n_half + c], preferred_element_type=F32)
            v_ref[r, c * W_COLS:(c + 1) * W_COLS] = _gelu(z)

    def ln_stage(r):
        v = v_ref[r, :]
        mu = jnp.mean(v, axis=-1, keepdims=True)
        xc = v - mu
        var = jnp.mean(xc * xc, axis=-1, keepdims=True)
        vn = xc * lax.rsqrt(var + LN_EPS) * ln_g_ref[...] + ln_b_ref[...]
        vn_ref[r, :] = vn.astype(BF16)

    def u_stage(r):
        for gp in range(n_half):
            u = _gelu(jnp.dot(n_ref[r, :], w_in_ref[gp], preferred_element_type=F32))
            for gg in range(2):
                g = 2 * gp + gg
                gl = slice(g * CHUNK, (g + 1) * CHUNK)
                for ci in range(SUB_GMLP // CHUNK):
                    rc = slice(r.start + ci * CHUNK, r.start + (ci + 1) * CHUNK)
                    sv = jnp.dot(wm_ref[g], vn_ref[rc, gl], preferred_element_type=F32) + bias_ref[g]
                    u_blk = u[ci * CHUNK:(ci + 1) * CHUNK, gg * CHUNK:(gg + 1) * CHUNK]
                    gated_ref[rc, gl] = (u_blk * sv).astype(BF16)

    def out_stage(r):
        m = _dot_col_tiles(gated_ref[r, :], w_out_ref)
        o_ref[r, :] = h_ref[r, :] + _rms(m, post_g_ref[...])

    def gate_stage(r):
        ln_stage(r)
        u_stage(r)

    _staggered(tiles, v_stage, gate_stage)
    for r in tiles:
        out_stage(r)


def _gmlp(h, pre_g, post_g, w_in, ln_g, ln_b, w_s, b_s, w_out, casts):
    t_total = h.shape[0]
    tm = TM_GMLP
    c_in, c_out, c_shapes = _cast_plan(casts, t_total // tm, lambda i: i)
    return pl.pallas_call(
        functools.partial(_gmlp_kernel, len(casts)),
        grid=(t_total // tm,),
        in_specs=[_row_spec(tm, D_MODEL), _const_spec((1, D_MODEL)), _const_spec((1, D_MODEL)),
                  _const_spec((2 * GMLP_HALF // W_COLS, D_MODEL, W_COLS)),
                  _const_spec((1, GMLP_HALF)), _const_spec((1, GMLP_HALF)),
                  _const_spec((GMLP_GROUPS, CHUNK, CHUNK)), _const_spec((CHUNK, GMLP_GROUPS)),
                  _const_spec((D_MODEL // W_COLS, GMLP_HALF, W_COLS))] + c_in,
        out_specs=[_row_spec(tm, D_MODEL)] + c_out,
        out_shape=[jax.ShapeDtypeStruct((t_total, D_MODEL), F32)] + c_shapes,
        scratch_shapes=[pltpu.VMEM((GMLP_GROUPS, CHUNK, CHUNK), BF16),
                        pltpu.VMEM((GMLP_GROUPS, CHUNK, CHUNK), F32),
                        pltpu.VMEM((tm, D_MODEL), BF16),
                        pltpu.VMEM((tm, GMLP_HALF), F32), pltpu.VMEM((tm, GMLP_HALF), BF16),
                        pltpu.VMEM((tm, GMLP_HALF), BF16)],
        compiler_params=_params("arbitrary"),
        name="gmlp",
    )(h, pre_g.reshape(1, D_MODEL), post_g.reshape(1, D_MODEL), w_in,
      ln_g.reshape(1, GMLP_HALF), ln_b.reshape(1, GMLP_HALF), w_s, b_s.T, w_out,
      *[c.w for c in casts])


def _attn_kernel(n_cast, *refs):
    ((qt_ref, k_ref, vt_ref), src, (o_ref,), dst) = _split(refs, 3, n_cast, 1, n_cast)
    _cast_blocks(src, dst)
    seq = k_ref.shape[1]
    n_blk = seq // TQ
    k_idx = lax.broadcasted_iota(jnp.int32, (TQ, TQ), 0)
    q_idx = lax.broadcasted_iota(jnp.int32, (TQ, TQ), 1)
    causal = k_idx <= q_idx
    vts = [vt_ref[hd * VT_ROWS:(hd + 1) * VT_ROWS, :] for hd in range(HEADS_PER_STEP)]

    def scores(hd, qi):
        return jnp.dot(k_ref[hd, :(qi + 1) * TQ, :],
                       qt_ref[hd * HEAD_PAD:(hd + 1) * HEAD_PAD, qi * TQ:(qi + 1) * TQ],
                       preferred_element_type=F32)

    tasks = [(hd, qi) for qi in range(n_blk) for hd in range(HEADS_PER_STEP)]
    pending = [scores(*t) for t in tasks[:HEADS_PER_STEP]]
    for k, (hd, qi) in enumerate(tasks):
        q0, q1 = qi * TQ, (qi + 1) * TQ
        s = pending.pop(0)
        if k + HEADS_PER_STEP < len(tasks):
            pending.append(scores(*tasks[k + HEADS_PER_STEP]))
        s_dg = jnp.where(causal, s[q0:q1, :], NEG_INF)
        m = jnp.max(s_dg, axis=0, keepdims=True)
        if qi > 0:
            m = jnp.maximum(m, jnp.max(s[:q0, :], axis=0, keepdims=True))
        p = jnp.exp2(s_dg - m).astype(BF16)
        if qi > 0:
            p = jnp.concatenate([jnp.exp2(s[:q0, :] - m).astype(BF16), p], axis=0)
        acc = jnp.dot(vts[hd][:, :q1], p, preferred_element_type=F32)
        o_t = acc[:V_DIM] / acc[V_DIM:V_DIM + 1]
        o_ref[q0:q1, hd * V_DIM:(hd + 1) * V_DIM] = o_t.T.astype(BF16)


def _attention(qt, k, vt, batch, seq, casts):
    hps = HEADS_PER_STEP
    groups = N_HEADS // hps
    c_in, c_out, c_shapes = _cast_plan(casts, batch * groups, lambda b, h: b * groups + h)
    out, *cast = pl.pallas_call(
        functools.partial(_attn_kernel, len(casts)),
        grid=(batch, groups),
        in_specs=[pl.BlockSpec((None, hps * HEAD_PAD, seq), lambda b, h: (b, h, 0)),
                  pl.BlockSpec((None, hps, seq, HEAD_PAD), lambda b, h: (b, h, 0, 0)),
                  pl.BlockSpec((None, hps * VT_ROWS, seq), lambda b, h: (b, h, 0))] + c_in,
        out_specs=[pl.BlockSpec((None, seq, hps * V_DIM), lambda b, h: (b, 0, h))] + c_out,
        out_shape=[jax.ShapeDtypeStruct((batch, seq, N_HEADS * V_DIM), BF16)] + c_shapes,
        compiler_params=_params("arbitrary", "arbitrary"),
        name="mla_attention",
    )(qt, k, vt, *[c.w for c in casts])
    return out.reshape(batch * seq, N_HEADS * V_DIM), cast


def _swap_halves(w):
    half = w.shape[-1] // 2
    return jnp.concatenate([w[..., half:], w[..., :half]], axis=-1)


def _kv_weights(w_dkv, w_ukv):
    w_c = w_dkv[:, :KV_RANK].astype(BF16)
    w_r = w_dkv[:, KV_RANK:]
    w_rope = jnp.concatenate([w_r, _swap_halves(w_r)], axis=1).astype(BF16)
    w4 = w_ukv.reshape(KV_RANK, N_HEADS, QK_NOPE + V_DIM)
    w_uk = _col_tiles(w4[:, :, :QK_NOPE].reshape(KV_RANK, N_HEADS * QK_NOPE))
    w_uv_t = w4[:, :, QK_NOPE:].reshape(KV_RANK, N_HEADS * V_DIM).T.astype(BF16)
    return w_c, w_rope, w_uk, w_uv_t


def _q_weights(w_uq):
    w4 = w_uq.reshape(Q_RANK, N_HEADS, QK_NOPE + QK_ROPE)
    w_r = w4[:, :, QK_NOPE:]
    w = jnp.concatenate([w4[:, :, :QK_NOPE], w_r, _swap_halves(w_r)], axis=-1)
    return w.reshape(Q_RANK, N_HEADS * HEAD_PAD).T.astype(BF16)


def kernel(x, positions, ffn_pre_g, ffn_post_g, ffn_w_gate, ffn_w_up, ffn_w_down, mix_pre_g, mix_post_g, gmlp_w_in, gmlp_ln_g, gmlp_ln_b, gmlp_w_s, gmlp_b_s, gmlp_w_out, kv_norm_g, w_dkv, kv_a_norm_g, w_ukv, mla_w_dq, mla_q_norm_g, mla_w_uq, mla_w_o):
    batch, seq, _ = x.shape
    h = x.reshape(batch * seq, D_MODEL)
    depth = ffn_pre_g.shape[0]
    gains = (ffn_pre_g.reshape(depth, 2, 1, D_MODEL), ffn_post_g.reshape(depth, 2, 1, D_MODEL))

    def ffn_casts(layer, j):
        return [_Cast(ffn_w_gate, (layer, j), False), _Cast(ffn_w_up, (layer, j), False),
                _Cast(ffn_w_down, (layer, j), True)]

    tab, tab_t, *w00 = _rope_table(positions, ffn_casts(0, 0))
    h, w_in, w_out = _ffn(h, gains, w00, 0, 0,
                          [_Cast(gmlp_w_in, (0,), True), _Cast(gmlp_w_out, (0,), True)])
    h, *w_ffn = _gmlp(h, mix_pre_g[0], mix_post_g[0], w_in, gmlp_ln_g[0], gmlp_ln_b[0],
                      gmlp_w_s[0], gmlp_b_s[0], w_out, ffn_casts(0, 1) + ffn_casts(1, 0))
    w01, w10 = w_ffn[:3], w_ffn[3:]
    w_c, w_rope, w_uk, w_uv_t = _kv_weights(w_dkv, w_ukv)
    h, k, vt = _ffn_kv(h, gains, w01, 0, 1, batch, seq, kv_norm_g, w_c, w_rope,
                       kv_a_norm_g, w_uk, w_uv_t, tab)
    h, qt = _ffn_q(h, gains, w10, 1, 0, batch, seq, mix_pre_g[1], mla_w_dq[0].astype(BF16),
                   mla_q_norm_g[0], _q_weights(mla_w_uq[0]), tab_t)
    att, w11 = _attention(qt, k, vt, batch, seq, ffn_casts(1, 1))
    h = _oproj_ffn(h, att, _col_tiles(mla_w_o[0]), mix_post_g[1], gains, w11, 1, 1)
    return h.reshape(batch, seq, D_MODEL)
```

```python
import functools
from typing import NamedTuple

import jax
import jax.numpy as jnp
from jax import lax
from jax.experimental import pallas as pl
from jax.experimental.pallas import tpu as pltpu

D_MODEL = 1024
D_FF = 2816
CHUNK = 128
GMLP_HALF = 2 * D_MODEL
GMLP_GROUPS = 16
N_HEADS = 8
QK_NOPE = 128
QK_ROPE = 64
V_DIM = 128
KV_RANK = 256
Q_RANK = 512
ROPE_THETA = 10000.0
RMS_EPS = 1e-6
LN_EPS = 1e-5
NEG_INF = -1e30
QK_SCALE_LOG2E = (QK_NOPE + QK_ROPE) ** -0.5 * 1.4426950408889634

HEAD_PAD = 256
FF_COLS = 256
W_COLS = 256
TM_FFN = 1024
SUB_FFN = 256
TM_GMLP = 1024
SUB_GMLP = 256
TM_FUSED = 1024
SUB_FUSED = 256
TM_ROPE = 8192
TQ = 256
HEADS_PER_STEP = 4
SCORES_AHEAD = 6
BF16_ROWS = 16
VT_ROWS = V_DIM + BF16_ROWS
VMEM_LIMIT = 56 * 1024 * 1024

BF16 = jnp.bfloat16
F32 = jnp.float32


def _rms(x, g):
    return x * lax.rsqrt(jnp.mean(x * x, axis=-1, keepdims=True) + RMS_EPS) * g


def _const_spec(shape):
    return pl.BlockSpec(shape, lambda *_: (0,) * len(shape), pipeline_mode=pl.Buffered(1))


def _row_spec(tm, width):
    return pl.BlockSpec((tm, width), lambda i: (i, 0))


def _row_tiles(rows, sub):
    return [slice(i * sub, (i + 1) * sub) for i in range(rows // sub)]


def _staggered(tiles, first, second):
    first(tiles[0])
    for prev, r in zip(tiles, tiles[1:]):
        first(r)
        second(prev)
    second(tiles[-1])


def _col_tiles(w):
    k, n = w.shape
    return jnp.swapaxes(w.reshape(k, n // W_COLS, W_COLS), 0, 1).astype(BF16)


def _dot_col_tiles(x, w_ref):
    return jnp.concatenate([jnp.dot(x, w_ref[c], preferred_element_type=F32)
                            for c in range(w_ref.shape[0])], axis=1)


def _params(*semantics):
    return pltpu.CompilerParams(dimension_semantics=semantics, vmem_limit_bytes=VMEM_LIMIT)


class _Cast(NamedTuple):
    w: jax.Array
    lead: tuple
    col_tiled: bool


def _cast_plan(jobs, n_steps, step_of):
    in_specs, out_specs, out_shapes = [], [], []
    for job in jobs:
        k, n = job.w.shape[-2:]
        chunks = max(c for c in range(1, n_steps + 1) if k % (BF16_ROWS * c) == 0)
        rows = k // chunks

        def chunk(*idx, chunks=chunks):
            return jnp.minimum(step_of(*idx), chunks - 1)

        in_specs.append(pl.BlockSpec((None,) * len(job.lead) + (rows, n),
                                     lambda *idx, lead=job.lead, chunk=chunk: lead + (chunk(*idx), 0)))
        if job.col_tiled:
            out_specs.append(pl.BlockSpec((n // W_COLS, rows, W_COLS),
                                          lambda *idx, chunk=chunk: (0, chunk(*idx), 0)))
            out_shapes.append(jax.ShapeDtypeStruct((n // W_COLS, k, W_COLS), BF16))
        else:
            out_specs.append(pl.BlockSpec((rows, n), lambda *idx, chunk=chunk: (chunk(*idx), 0)))
            out_shapes.append(jax.ShapeDtypeStruct((k, n), BF16))
    return in_specs, out_specs, out_shapes


def _cast_blocks(src_refs, dst_refs):
    for src, dst in zip(src_refs, dst_refs):
        if len(dst.shape) == 3:
            for c in range(dst.shape[0]):
                dst[c] = src[:, c * W_COLS:(c + 1) * W_COLS].astype(BF16)
        else:
            dst[...] = src[...].astype(BF16)


def _split(refs, *counts):
    out, i = [], 0
    for c in counts:
        out.append(refs[i:i + c])
        i += c
    assert i == len(refs)
    return out


def _rope_tab_kernel(n_cast, *refs):
    (pos_ref, invf_ref), src, (tab_ref, tab_t_ref), dst = _split(refs, 2, n_cast, 2, n_cast)
    _cast_blocks(src, dst)
    ang = invf_ref[...] * pos_ref[...]
    c = jnp.cos(ang)
    s = jnp.sin(ang)
    t = jnp.concatenate([c, c, -s, s], axis=0)
    tab_t_ref[...] = t
    tab_ref[...] = t.T


def _rope_table(positions, casts):
    t_total = positions.size
    tm = TM_ROPE
    pos = positions.reshape(1, t_total).astype(F32)
    inv_freq = ROPE_THETA ** (-jnp.arange(0, QK_ROPE, 2, dtype=F32) / QK_ROPE)
    c_in, c_out, c_shapes = _cast_plan(casts, t_total // tm, lambda i: i)
    return pl.pallas_call(
        functools.partial(_rope_tab_kernel, len(casts)),
        grid=(t_total // tm,),
        in_specs=[pl.BlockSpec((1, tm), lambda i: (0, i)), _const_spec((QK_ROPE // 2, 1))] + c_in,
        out_specs=[pl.BlockSpec((tm, 2 * QK_ROPE), lambda i: (i, 0)),
                   pl.BlockSpec((2 * QK_ROPE, tm), lambda i: (0, i))] + c_out,
        out_shape=[jax.ShapeDtypeStruct((t_total, 2 * QK_ROPE), F32),
                   jax.ShapeDtypeStruct((2 * QK_ROPE, t_total), F32)] + c_shapes,
        compiler_params=_params("arbitrary"),
        name="rope_table",
    )(pos, inv_freq.reshape(QK_ROPE // 2, 1), *[c.w for c in casts])


def _rope(x2, tab):
    prod = x2 * tab
    return prod + pltpu.roll(prod, QK_ROPE, axis=1)


def _ffn_up(x, pre_g_ref, wg_ref, wu_ref, a_ref, r):
    n = _rms(x, pre_g_ref[...]).astype(BF16)
    for c in range(D_FF // FF_COLS):
        sl = slice(c * FF_COLS, (c + 1) * FF_COLS)
        g = jnp.dot(n, wg_ref[:, sl], preferred_element_type=F32)
        u = jnp.dot(n, wu_ref[:, sl], preferred_element_type=F32)
        a_ref[r, sl] = (g * jax.nn.sigmoid(g) * u).astype(BF16)


def _ffn_down(x, post_g_ref, wd_ref, a_ref, r):
    f = _dot_col_tiles(a_ref[r, :], wd_ref)
    return x + 0.5 * _rms(f, post_g_ref[...])


_NT = (((1,), (1,)), ((), ()))


def _kv_stage(xs, tiles, g_ref, w_c_ref, w_rope_ref, ga_ref, w_uk_ref, w_uv_t_ref, tab_ref,
              k_ref, vt_ref):
    ns = [_rms(x, g_ref[...]).astype(BF16) for x in xs]
    cs = [_rms(jnp.dot(n, w_c_ref[...], preferred_element_type=F32), ga_ref[...]).astype(BF16)
          for n in ns]
    ones = jnp.ones((BF16_ROWS, tiles[0].stop - tiles[0].start), BF16)
    for n, c, r in zip(ns, cs, tiles):
        k = _dot_col_tiles(c, w_uk_ref)
        v_t = lax.dot_general(w_uv_t_ref[...], c, _NT, preferred_element_type=F32)
        k2 = jnp.dot(n, w_rope_ref[...], preferred_element_type=F32)
        kr = _rope(k2, tab_ref[r, :]).astype(BF16)
        for hh in range(N_HEADS):
            k_ref[hh, r, :QK_NOPE] = k[:, hh * QK_NOPE:(hh + 1) * QK_NOPE].astype(BF16)
            k_ref[hh, r, QK_NOPE:] = kr
            vt_ref[hh * VT_ROWS:hh * VT_ROWS + V_DIM, r] = (
                v_t[hh * V_DIM:(hh + 1) * V_DIM].astype(BF16))
            vt_ref[hh * VT_ROWS + V_DIM:(hh + 1) * VT_ROWS, r] = ones


def _q_stage(xs, tiles, g_ref, w_dq_ref, gq_ref, w_uq_t_ref, tab_t_ref, qt_ref):
    ns = [_rms(x, g_ref[...]).astype(BF16) for x in xs]
    qns = [_rms(jnp.dot(n, w_dq_ref[...], preferred_element_type=F32), gq_ref[...]).astype(BF16)
           for n in ns]
    for qn, r in zip(qns, tiles):
        q_t = lax.dot_general(w_uq_t_ref[...], qn, _NT, preferred_element_type=F32)
        q_t = q_t * QK_SCALE_LOG2E
        tab_t = tab_t_ref[:, r]
        pad = jnp.zeros((HEAD_PAD - QK_NOPE - QK_ROPE, q_t.shape[1]), BF16)
        for hh in range(N_HEADS):
            base = hh * HEAD_PAD
            qt_ref[base:base + QK_NOPE, r] = q_t[base:base + QK_NOPE].astype(BF16)
            prod = q_t[base + QK_NOPE:base + HEAD_PAD] * tab_t
            rope_t = prod[:QK_ROPE] + prod[QK_ROPE:]
            qt_ref[base + QK_NOPE:base + QK_NOPE + QK_ROPE, r] = rope_t.astype(BF16)
            qt_ref[base + QK_NOPE + QK_ROPE:base + HEAD_PAD, r] = pad


def _ffn_kernel(n_cast, *refs):
    ((h_ref, pre_g_ref, post_g_ref, wg_ref, wu_ref, wd_ref), src, (o_ref,), dst,
     (a_ref,)) = _split(refs, 6, n_cast, 1, n_cast, 1)
    _cast_blocks(src, dst)
    tiles = _row_tiles(h_ref.shape[0], SUB_FFN)
    for r in tiles:
        _ffn_up(h_ref[r, :], pre_g_ref, wg_ref, wu_ref, a_ref, r)
    for r in tiles:
        o_ref[r, :] = _ffn_down(h_ref[r, :], post_g_ref, wd_ref, a_ref, r)


def _ffn_kv_kernel(h_ref, pre_g_ref, post_g_ref, wg_ref, wu_ref, wd_ref,
                   g_ref, w_c_ref, w_rope_ref, ga_ref, w_uk_ref, w_uv_t_ref, tab_ref,
                   o_ref, k_ref, vt_ref, a_ref):
    tiles = _row_tiles(h_ref.shape[0], SUB_FUSED)
    for r in tiles:
        _ffn_up(h_ref[r, :], pre_g_ref, wg_ref, wu_ref, a_ref, r)
    for r in tiles:
        o_ref[r, :] = _ffn_down(h_ref[r, :], post_g_ref, wd_ref, a_ref, r)
    _kv_stage([o_ref[r, :] for r in tiles], tiles, g_ref, w_c_ref, w_rope_ref, ga_ref,
              w_uk_ref, w_uv_t_ref, tab_ref, k_ref, vt_ref)


def _ffn_q_kernel(h_ref, pre_g_ref, post_g_ref, wg_ref, wu_ref, wd_ref,
                  g_ref, w_dq_ref, gq_ref, w_uq_t_ref, tab_t_ref, o_ref, qt_ref, a_ref):
    tiles = _row_tiles(h_ref.shape[0], SUB_FUSED)
    for r in tiles:
        _ffn_up(h_ref[r, :], pre_g_ref, wg_ref, wu_ref, a_ref, r)
    for r in tiles:
        o_ref[r, :] = _ffn_down(h_ref[r, :], post_g_ref, wd_ref, a_ref, r)
    _q_stage([o_ref[r, :] for r in tiles], tiles, g_ref, w_dq_ref, gq_ref, w_uq_t_ref, tab_t_ref,
             qt_ref)


def _oproj_ffn_kernel(h_ref, att_ref, w_o_ref, g_mix_ref, pre_g_ref, post_g_ref, wg_ref, wu_ref,
                      wd_ref, o_ref, a_ref, h1_ref):
    tiles = _row_tiles(h_ref.shape[0], SUB_FUSED)
    def project(r):
        m = _dot_col_tiles(att_ref[r, :], w_o_ref)
        h1_ref[r, :] = h_ref[r, :] + _rms(m, g_mix_ref[...])

    def up(r):
        _ffn_up(h1_ref[r, :], pre_g_ref, wg_ref, wu_ref, a_ref, r)

    _staggered(tiles, project, up)
    for r in tiles:
        o_ref[r, :] = _ffn_down(h1_ref[r, :], post_g_ref, wd_ref, a_ref, r)


def _ffn_weight_specs(layer, j):
    gain = pl.BlockSpec((None, None, 1, D_MODEL), lambda *_: (layer, j, 0, 0),
                        pipeline_mode=pl.Buffered(1))
    return [gain, gain, _const_spec((D_MODEL, D_FF)), _const_spec((D_MODEL, D_FF)),
            _const_spec((D_MODEL // W_COLS, D_FF, W_COLS))]


def _ffn(h, gains, w, layer, j, casts):
    t_total = h.shape[0]
    tm = TM_FFN
    c_in, c_out, c_shapes = _cast_plan(casts, t_total // tm, lambda i: i)
    return pl.pallas_call(
        functools.partial(_ffn_kernel, len(casts)),
        grid=(t_total // tm,),
        in_specs=[_row_spec(tm, D_MODEL)] + _ffn_weight_specs(layer, j) + c_in,
        out_specs=[_row_spec(tm, D_MODEL)] + c_out,
        out_shape=[jax.ShapeDtypeStruct((t_total, D_MODEL), F32)] + c_shapes,
        scratch_shapes=[pltpu.VMEM((tm, D_FF), BF16)],
        compiler_params=_params("arbitrary"),
        name="ffn",
    )(h, *gains, *w, *[c.w for c in casts])


def _seq_cols_spec(rows, tm, per_b):
    return pl.BlockSpec((None, rows, tm), lambda i: (i // per_b, 0, i % per_b))


def _ffn_kv(h, gains, w, layer, j, batch, seq, g, w_c, w_rope, ga, w_uk, w_uv_t, tab):
    t_total = h.shape[0]
    tm = TM_FUSED
    return pl.pallas_call(
        _ffn_kv_kernel,
        grid=(t_total // tm,),
        in_specs=[_row_spec(tm, D_MODEL)] + _ffn_weight_specs(layer, j) + [
            _const_spec((1, D_MODEL)), _const_spec((D_MODEL, KV_RANK)),
            _const_spec((D_MODEL, 2 * QK_ROPE)), _const_spec((1, KV_RANK)),
            _const_spec((N_HEADS * QK_NOPE // W_COLS, KV_RANK, W_COLS)),
            _const_spec((N_HEADS * V_DIM, KV_RANK)),
            _row_spec(tm, 2 * QK_ROPE)],
        out_specs=[_row_spec(tm, D_MODEL),
                   pl.BlockSpec((None, N_HEADS, tm, HEAD_PAD),
                                lambda i: (i // (seq // tm), 0, i % (seq // tm), 0)),
                   _seq_cols_spec(N_HEADS * VT_ROWS, tm, seq // tm)],
        out_shape=[jax.ShapeDtypeStruct((t_total, D_MODEL), F32),
                   jax.ShapeDtypeStruct((batch, N_HEADS, seq, HEAD_PAD), BF16),
                   jax.ShapeDtypeStruct((batch, N_HEADS * VT_ROWS, seq), BF16)],
        scratch_shapes=[pltpu.VMEM((tm, D_FF), BF16)],
        compiler_params=_params("parallel"),
        name="ffn_kv",
    )(h, *gains, *w, g.reshape(1, D_MODEL), w_c, w_rope, ga.reshape(1, KV_RANK), w_uk, w_uv_t, tab)


def _ffn_q(h, gains, w, layer, j, batch, seq, g, w_dq, gq, w_uq_t, tab_t):
    t_total = h.shape[0]
    tm = TM_FUSED
    return pl.pallas_call(
        _ffn_q_kernel,
        grid=(t_total // tm,),
        in_specs=[_row_spec(tm, D_MODEL)] + _ffn_weight_specs(layer, j) + [
            _const_spec((1, D_MODEL)), _const_spec((D_MODEL, Q_RANK)), _const_spec((1, Q_RANK)),
            _const_spec((N_HEADS * HEAD_PAD, Q_RANK)),
            pl.BlockSpec((2 * QK_ROPE, tm), lambda i: (0, i))],
        out_specs=[_row_spec(tm, D_MODEL), _seq_cols_spec(N_HEADS * HEAD_PAD, tm, seq // tm)],
        out_shape=[jax.ShapeDtypeStruct((t_total, D_MODEL), F32),
                   jax.ShapeDtypeStruct((batch, N_HEADS * HEAD_PAD, seq), BF16)],
        scratch_shapes=[pltpu.VMEM((tm, D_FF), BF16)],
        compiler_params=_params("parallel"),
        name="ffn_q",
    )(h, *gains, *w, g.reshape(1, D_MODEL), w_dq, gq.reshape(1, Q_RANK), w_uq_t, tab_t)


def _oproj_ffn(h, att, w_o, g_mix, gains, w, layer, j):
    t_total = h.shape[0]
    tm = TM_FUSED
    return pl.pallas_call(
        _oproj_ffn_kernel,
        grid=(t_total // tm,),
        in_specs=[_row_spec(tm, D_MODEL), _row_spec(tm, N_HEADS * V_DIM),
                  _const_spec((D_MODEL // W_COLS, N_HEADS * V_DIM, W_COLS)),
                  _const_spec((1, D_MODEL))] + _ffn_weight_specs(layer, j),
        out_specs=_row_spec(tm, D_MODEL),
        out_shape=jax.ShapeDtypeStruct((t_total, D_MODEL), F32),
        scratch_shapes=[pltpu.VMEM((tm, D_FF), BF16), pltpu.VMEM((tm, D_MODEL), F32)],
        compiler_params=_params("parallel"),
        name="oproj_ffn",
    )(h, att, w_o, g_mix.reshape(1, D_MODEL), *gains, *w)


_GELU_A = -2.0 * 0.7978845608028654 * 1.4426950408889634
_GELU_B = _GELU_A * 0.044715


def _gelu(x):
    return x / (1.0 + jnp.exp2(x * (_GELU_A + _GELU_B * (x * x))))


def _gmlp_kernel(n_cast, *refs):
    ((h_ref, pre_g_ref, post_g_ref, w_in_ref, ln_g_ref, ln_b_ref, w_s_ref, b_st_ref, w_out_ref),
     src, (o_ref,), dst, (wm_ref, bias_ref, n_ref, v_ref, vn_ref, gated_ref)) = _split(
         refs, 9, n_cast, 1, n_cast, 6)
    _cast_blocks(src, dst)

    @pl.when(pl.program_id(0) == 0)
    def _():
        t_idx = lax.broadcasted_iota(jnp.int32, (CHUNK, CHUNK), 0)
        c_idx = lax.broadcasted_iota(jnp.int32, (CHUNK, CHUNK), 1)
        for g in range(GMLP_GROUPS):
            wm_ref[g] = jnp.where(c_idx <= t_idx, w_s_ref[g], 0.0).astype(BF16)
            bias_ref[g] = jnp.broadcast_to(b_st_ref[:, g:g + 1], (CHUNK, CHUNK))

    tiles = _row_tiles(h_ref.shape[0], SUB_GMLP)
    n_half = GMLP_HALF // W_COLS

    def v_stage(r):
        n_ref[r, :] = _rms(h_ref[r, :], pre_g_ref[...]).astype(BF16)
        for c in range(n_half):
            z = jnp.dot(n_ref[r, :], w_in_ref[n_half + c], preferred_element_type=F32)
            v_ref[r, c * W_COLS:(c + 1) * W_COLS] = _gelu(z)

    def ln_stage(r):
        v = v_ref[r, :]
        mu = jnp.mean(v, axis=-1, keepdims=True)
        xc = v - mu
        var = jnp.mean(xc * xc, axis=-1, keepdims=True)
        vn = xc * lax.rsqrt(var + LN_EPS) * ln_g_ref[...] + ln_b_ref[...]
        vn_ref[r, :] = vn.astype(BF16)

    def u_stage(r):
        for gp in range(n_half):
            u = _gelu(jnp.dot(n_ref[r, :], w_in_ref[gp], preferred_element_type=F32))
            for gg in range(2):
                g = 2 * gp + gg
                gl = slice(g * CHUNK, (g + 1) * CHUNK)
                for ci in range(SUB_GMLP // CHUNK):
                    rc = slice(r.start + ci * CHUNK, r.start + (ci + 1) * CHUNK)
                    sv = jnp.dot(wm_ref[g], vn_ref[rc, gl], preferred_element_type=F32) + bias_ref[g]
                    u_blk = u[ci * CHUNK:(ci + 1) * CHUNK, gg * CHUNK:(gg + 1) * CHUNK]
                    gated_ref[rc, gl] = (u_blk * sv).astype(BF16)

    def out_stage(r):
        m = _dot_col_tiles(gated_ref[r, :], w_out_ref)
        o_ref[r, :] = h_ref[r, :] + _rms(m, post_g_ref[...])

    def gate_stage(r):
        ln_stage(r)
        u_stage(r)

    _staggered(tiles, v_stage, gate_stage)
    for r in tiles:
        out_stage(r)


def _gmlp(h, pre_g, post_g, w_in, ln_g, ln_b, w_s, b_s, w_out, casts):
    t_total = h.shape[0]
    tm = TM_GMLP
    c_in, c_out, c_shapes = _cast_plan(casts, t_total // tm, lambda i: i)
    return pl.pallas_call(
        functools.partial(_gmlp_kernel, len(casts)),
        grid=(t_total // tm,),
        in_specs=[_row_spec(tm, D_MODEL), _const_spec((1, D_MODEL)), _const_spec((1, D_MODEL)),
                  _const_spec((2 * GMLP_HALF // W_COLS, D_MODEL, W_COLS)),
                  _const_spec((1, GMLP_HALF)), _const_spec((1, GMLP_HALF)),
                  _const_spec((GMLP_GROUPS, CHUNK, CHUNK)), _const_spec((CHUNK, GMLP_GROUPS)),
                  _const_spec((D_MODEL // W_COLS, GMLP_HALF, W_COLS))] + c_in,
        out_specs=[_row_spec(tm, D_MODEL)] + c_out,
        out_shape=[jax.ShapeDtypeStruct((t_total, D_MODEL), F32)] + c_shapes,
        scratch_shapes=[pltpu.VMEM((GMLP_GROUPS, CHUNK, CHUNK), BF16),
                        pltpu.VMEM((GMLP_GROUPS, CHUNK, CHUNK), F32),
                        pltpu.VMEM((tm, D_MODEL), BF16),
                        pltpu.VMEM((tm, GMLP_HALF), F32), pltpu.VMEM((tm, GMLP_HALF), BF16),
                        pltpu.VMEM((tm, GMLP_HALF), BF16)],
        compiler_params=_params("arbitrary"),
        name="gmlp",
    )(h, pre_g.reshape(1, D_MODEL), post_g.reshape(1, D_MODEL), w_in,
      ln_g.reshape(1, GMLP_HALF), ln_b.reshape(1, GMLP_HALF), w_s, b_s.T, w_out,
      *[c.w for c in casts])


def _attn_kernel(n_cast, *refs):
    ((qt_ref, k_ref, vt_ref), src, (o_ref,), dst) = _split(refs, 3, n_cast, 1, n_cast)
    _cast_blocks(src, dst)
    seq = k_ref.shape[1]
    n_blk = seq // TQ
    k_idx = lax.broadcasted_iota(jnp.int32, (TQ, TQ), 0)
    q_idx = lax.broadcasted_iota(jnp.int32, (TQ, TQ), 1)
    causal = k_idx <= q_idx
    vts = [vt_ref[hd * VT_ROWS:(hd + 1) * VT_ROWS, :] for hd in range(HEADS_PER_STEP)]

    def scores(hd, qi):
        return jnp.dot(k_ref[hd, :(qi + 1) * TQ, :],
                       qt_ref[hd * HEAD_PAD:(hd + 1) * HEAD_PAD, qi * TQ:(qi + 1) * TQ],
                       preferred_element_type=F32)

    tasks = [(hd, qi) for qi in range(n_blk) for hd in range(HEADS_PER_STEP)]
    pending = [scores(*t) for t in tasks[:SCORES_AHEAD]]
    for k, (hd, qi) in enumerate(tasks):
        q0, q1 = qi * TQ, (qi + 1) * TQ
        s = pending.pop(0)
        if k + SCORES_AHEAD < len(tasks):
            pending.append(scores(*tasks[k + SCORES_AHEAD]))
        s_dg = jnp.where(causal, s[q0:q1, :], NEG_INF)
        m = jnp.max(s_dg, axis=0, keepdims=True)
        if qi > 0:
            m = jnp.maximum(m, jnp.max(s[:q0, :], axis=0, keepdims=True))
        p = jnp.exp2(s_dg - m).astype(BF16)
        if qi > 0:
            p = jnp.concatenate([jnp.exp2(s[:q0, :] - m).astype(BF16), p], axis=0)
        acc = jnp.dot(vts[hd][:, :q1], p, preferred_element_type=F32)
        o_t = acc[:V_DIM] / acc[V_DIM:V_DIM + 1]
        o_ref[q0:q1, hd * V_DIM:(hd + 1) * V_DIM] = o_t.T.astype(BF16)


def _attention(qt, k, vt, batch, seq, casts):
    hps = HEADS_PER_STEP
    groups = N_HEADS // hps
    c_in, c_out, c_shapes = _cast_plan(casts, batch * groups, lambda b, h: b * groups + h)
    out, *cast = pl.pallas_call(
        functools.partial(_attn_kernel, len(casts)),
        grid=(batch, groups),
        in_specs=[pl.BlockSpec((None, hps * HEAD_PAD, seq), lambda b, h: (b, h, 0)),
                  pl.BlockSpec((None, hps, seq, HEAD_PAD), lambda b, h: (b, h, 0, 0)),
                  pl.BlockSpec((None, hps * VT_ROWS, seq), lambda b, h: (b, h, 0))] + c_in,
        out_specs=[pl.BlockSpec((None, seq, hps * V_DIM), lambda b, h: (b, 0, h))] + c_out,
        out_shape=[jax.ShapeDtypeStruct((batch, seq, N_HEADS * V_DIM), BF16)] + c_shapes,
        compiler_params=_params("arbitrary", "arbitrary"),
        name="mla_attention",
    )(qt, k, vt, *[c.w for c in casts])
    return out.reshape(batch * seq, N_HEADS * V_DIM), cast


def _swap_halves(w):
    half = w.shape[-1] // 2
    return jnp.concatenate([w[..., half:], w[..., :half]], axis=-1)


def _kv_weights(w_dkv, w_ukv):
    w_c = w_dkv[:, :KV_RANK].astype(BF16)
    w_r = w_dkv[:, KV_RANK:]
    w_rope = jnp.concatenate([w_r, _swap_halves(w_r)], axis=1).astype(BF16)
    w4 = w_ukv.reshape(KV_RANK, N_HEADS, QK_NOPE + V_DIM)
    w_uk = _col_tiles(w4[:, :, :QK_NOPE].reshape(KV_RANK, N_HEADS * QK_NOPE))
    w_uv_t = w4[:, :, QK_NOPE:].reshape(KV_RANK, N_HEADS * V_DIM).T.astype(BF16)
    return w_c, w_rope, w_uk, w_uv_t


def _q_weights(w_uq):
    w4 = w_uq.reshape(Q_RANK, N_HEADS, QK_NOPE + QK_ROPE)
    w_r = w4[:, :, QK_NOPE:]
    w = jnp.concatenate([w4[:, :, :QK_NOPE], w_r, _swap_halves(w_r)], axis=-1)
    return w.reshape(Q_RANK, N_HEADS * HEAD_PAD).T.astype(BF16)


def kernel(x, positions, ffn_pre_g, ffn_post_g, ffn_w_gate, ffn_w_up, ffn_w_down, mix_pre_g, mix_post_g, gmlp_w_in, gmlp_ln_g, gmlp_ln_b, gmlp_w_s, gmlp_b_s, gmlp_w_out, kv_norm_g, w_dkv, kv_a_norm_g, w_ukv, mla_w_dq, mla_q_norm_g, mla_w_uq, mla_w_o):
    batch, seq, _ = x.shape
    h = x.reshape(batch * seq, D_MODEL)
    depth = ffn_pre_g.shape[0]
    gains = (ffn_pre_g.reshape(depth, 2, 1, D_MODEL), ffn_post_g.reshape(depth, 2, 1, D_MODEL))

    def ffn_casts(layer, j):
        return [_Cast(ffn_w_gate, (layer, j), False), _Cast(ffn_w_up, (layer, j), False),
                _Cast(ffn_w_down, (layer, j), True)]

    tab, tab_t, *w00 = _rope_table(positions, ffn_casts(0, 0))
    h, w_in, w_out = _ffn(h, gains, w00, 0, 0,
                          [_Cast(gmlp_w_in, (0,), True), _Cast(gmlp_w_out, (0,), True)])
    h, *w_ffn = _gmlp(h, mix_pre_g[0], mix_post_g[0], w_in, gmlp_ln_g[0], gmlp_ln_b[0],
                      gmlp_w_s[0], gmlp_b_s[0], w_out, ffn_casts(0, 1) + ffn_casts(1, 0))
    w01, w10 = w_ffn[:3], w_ffn[3:]
    w_c, w_rope, w_uk, w_uv_t = _kv_weights(w_dkv, w_ukv)
    h, k, vt = _ffn_kv(h, gains, w01, 0, 1, batch, seq, kv_norm_g, w_c, w_rope,
                       kv_a_norm_g, w_uk, w_uv_t, tab)
    h, qt = _ffn_q(h, gains, w10, 1, 0, batch, seq, mix_pre_g[1], mla_w_dq[0].astype(BF16),
                   mla_q_norm_g[0], _q_weights(mla_w_uq[0]), tab_t)
    att, w11 = _attention(qt, k, vt, batch, seq, ffn_casts(1, 1))
    h = _oproj_ffn(h, att, _col_tiles(mla_w_o[0]), mix_post_g[1], gains, w11, 1, 1)
    return h.reshape(batch, seq, D_MODEL)
```
